```python
import math
import jax, jax.numpy as jnp
from jax import lax
import numpy as np

D_MODEL = 2048
BATCH = 4
SEQ = 8192
DEPTH = 2

N_EVEN = (DEPTH + 1) // 2
N_ODD = DEPTH // 2

CONV_DIM = D_MODEL // 2
CONV_GROUPS = 16
CONV_WIDTH = 3
DIFF_HEADS = 8
DIFF_D = (D_MODEL // 2) // (2 * DIFF_HEADS)
DIFF_V = 2 * DIFF_D
DIFF_QK = DIFF_HEADS * 2 * DIFF_D
HYB_IN = 3 * CONV_DIM + 2 * DIFF_QK + DIFF_HEADS * DIFF_V
HYB_OUT = CONV_DIM + DIFF_HEADS * DIFF_V
FOX_HEADS = 16
FOX_D = D_MODEL // FOX_HEADS
FOX_IN = 3 * D_MODEL + FOX_HEADS
D_FF = 4 * D_MODEL

Q_BLOCK = 128
EPS = 1e-6

kernel_name = "hybrid_shortconv_diffattn_fox_sqrelu"


def _rmsnorm(x, g):
    xf = x.astype(jnp.float32)
    y = xf * lax.rsqrt(jnp.mean(xf * xf, axis=-1, keepdims=True) + EPS)
    return (y * g.astype(jnp.float32)).astype(x.dtype)


def _split(a, sizes):
    idx = [int(s) for s in np.cumsum(sizes)[:-1]]
    return jnp.split(a, idx, axis=-1)


def _causal_conv(u, w):
    c = u.shape[-1]
    return lax.conv_general_dilated(
        u, w[:, None, :].astype(u.dtype), window_strides=(1,),
        padding=[(CONV_WIDTH - 1, 0)], dimension_numbers=("NWC", "WIO", "NWC"),
        feature_group_count=c)


def _causal_mask(i, seq):
    q_pos = i * Q_BLOCK + jnp.arange(Q_BLOCK)
    k_pos = jnp.arange(seq)
    return k_pos[None, :] <= q_pos[:, None]


def _block_sweep(fn, seq):
    out = lax.map(fn, jnp.arange(seq // Q_BLOCK))
    nb, b, qb, h, e = out.shape
    return jnp.moveaxis(out, 0, 1).reshape(b, nb * qb, h, e)


def _diff_attention(q, k, v, lam, seq):
    scale = DIFF_D ** -0.5

    def block(i):
        qb = lax.dynamic_slice_in_dim(q, i * Q_BLOCK, Q_BLOCK, axis=1)
        s = jnp.einsum("bqhmd,bkhmd->bhmqk", qb, k,
                       preferred_element_type=jnp.float32) * scale
        s = jnp.where(_causal_mask(i, seq), s, -jnp.inf)
        p = jax.nn.softmax(s, axis=-1)
        pd = p[:, :, 0] - lam * p[:, :, 1]
        return jnp.einsum("bhqk,bkhe->bqhe", pd.astype(v.dtype), v)

    return _block_sweep(block, seq)


def _forgetting_attention(q, k, v, logf, seq):
    c = jnp.transpose(jnp.cumsum(logf, axis=1), (0, 2, 1))
    scale = FOX_D ** -0.5

    def block(i):
        qb = lax.dynamic_slice_in_dim(q, i * Q_BLOCK, Q_BLOCK, axis=1)
        c_q = lax.dynamic_slice_in_dim(c, i * Q_BLOCK, Q_BLOCK, axis=2)
        s = jnp.einsum("bqhd,bkhd->bhqk", qb, k,
                       preferred_element_type=jnp.float32) * scale
        s = s + (c_q[..., :, None] - c[..., None, :])
        s = jnp.where(_causal_mask(i, seq), s, -jnp.inf)
        p = jax.nn.softmax(s, axis=-1)
        return jnp.einsum("bhqk,bkhd->bqhd", p.astype(v.dtype), v)

    return _block_sweep(block, seq)


def _conv_diff_mixer(h, w_in, conv_w, dq_g, dk_g, lq1, lk1, lq2, lk2, subln_g, w_out, layer_idx):
    b, t, _ = h.shape
    proj = h @ w_in
    gate_b, gate_c, u, q, k, v = _split(
        proj, [CONV_DIM, CONV_DIM, CONV_DIM, DIFF_QK, DIFF_QK, DIFF_HEADS * DIFF_V])
    y_conv = gate_b * _causal_conv(gate_c * u, conv_w)
    q = _rmsnorm(q.reshape(b, t, DIFF_HEADS, 2, DIFF_D), dq_g)
    k = _rmsnorm(k.reshape(b, t, DIFF_HEADS, 2, DIFF_D), dk_g)
    v = v.reshape(b, t, DIFF_HEADS, DIFF_V)
    lam_init = 0.8 - 0.6 * math.exp(-0.3 * layer_idx)
    f32 = jnp.float32
    lam = (jnp.exp(jnp.sum(lq1.astype(f32) * lk1.astype(f32)))
           - jnp.exp(jnp.sum(lq2.astype(f32) * lk2.astype(f32))) + lam_init)
    o = _diff_attention(q, k, v, lam, t)
    o = _rmsnorm(o, subln_g) * (1.0 - lam_init)
    o = o.astype(h.dtype).reshape(b, t, DIFF_HEADS * DIFF_V)
    return jnp.concatenate([y_conv, o], axis=-1) @ w_out


def _fox_mixer(h, w_in, b_f, q_g, k_g, w_out):
    b, t, _ = h.shape
    proj = h @ w_in
    q, k, v, f = _split(proj, [D_MODEL, D_MODEL, D_MODEL, FOX_HEADS])
    q = _rmsnorm(q.reshape(b, t, FOX_HEADS, FOX_D), q_g)
    k = _rmsnorm(k.reshape(b, t, FOX_HEADS, FOX_D), k_g)
    v = v.reshape(b, t, FOX_HEADS, FOX_D)
    logf = jax.nn.log_sigmoid(f.astype(jnp.float32) + b_f.astype(jnp.float32))
    o = _forgetting_attention(q, k, v, logf, t)
    return o.reshape(b, t, D_MODEL) @ w_out


def _sqrelu_mlp(h, w1, w2):
    return jnp.square(jax.nn.relu(h @ w1)) @ w2


def setup_inputs(seed: int = 0) -> dict:
    key = jax.random.key(seed)
    ks = jax.random.split(key, 20)
    nrm = jax.random.normal
    f32 = jnp.float32
    D = D_MODEL
    return {
        "x": nrm(ks[0], (BATCH, SEQ, D), f32),
        "norm1_g": 1.0 + 0.02 * nrm(ks[1], (DEPTH, D), f32),
        "norm2_g": 1.0 + 0.02 * nrm(ks[2], (DEPTH, D), f32),
        "hyb_w_in": nrm(ks[3], (N_EVEN, D, HYB_IN), f32) * D ** -0.5,
        "hyb_conv_w": nrm(ks[4], (N_EVEN, CONV_WIDTH, CONV_DIM), f32) * CONV_WIDTH ** -0.5,
        "hyb_dq_g": 1.0 + 0.02 * nrm(ks[5], (N_EVEN, DIFF_D), f32),
        "hyb_dk_g": 1.0 + 0.02 * nrm(ks[6], (N_EVEN, DIFF_D), f32),
        "hyb_lq1": 0.1 * nrm(ks[7], (N_EVEN, DIFF_D), f32),
        "hyb_lk1": 0.1 * nrm(ks[8], (N_EVEN, DIFF_D), f32),
        "hyb_lq2": 0.1 * nrm(ks[9], (N_EVEN, DIFF_D), f32),
        "hyb_lk2": 0.1 * nrm(ks[10], (N_EVEN, DIFF_D), f32),
        "hyb_subln_g": 1.0 + 0.02 * nrm(ks[11], (N_EVEN, DIFF_V), f32),
        "hyb_w_out": nrm(ks[12], (N_EVEN, HYB_OUT, D), f32) * HYB_OUT ** -0.5,
        "fox_w_in": nrm(ks[13], (N_ODD, D, FOX_IN), f32) * D ** -0.5,
        "fox_b_f": 3.0 + 0.5 * nrm(ks[14], (N_ODD, FOX_HEADS), f32),
        "fox_q_g": 1.0 + 0.02 * nrm(ks[15], (N_ODD, FOX_D), f32),
        "fox_k_g": 1.0 + 0.02 * nrm(ks[16], (N_ODD, FOX_D), f32),
        "fox_w_out": nrm(ks[17], (N_ODD, D, D), f32) * D ** -0.5,
        "mlp_w1": nrm(ks[18], (DEPTH, D, D_FF), f32) * D ** -0.5,
        "mlp_w2": nrm(ks[19], (DEPTH, D_FF, D), f32) * D_FF ** -0.5,
    }


def reference(x, norm1_g, norm2_g, hyb_w_in, hyb_conv_w, hyb_dq_g, hyb_dk_g,
              hyb_lq1, hyb_lk1, hyb_lq2, hyb_lk2, hyb_subln_g, hyb_w_out,
              fox_w_in, fox_b_f, fox_q_g, fox_k_g, fox_w_out, mlp_w1, mlp_w2):
    for l in range(DEPTH):
        h = _rmsnorm(x, norm1_g[l])
        j = l // 2
        if l % 2 == 0:
            x = x + _conv_diff_mixer(h, hyb_w_in[j], hyb_conv_w[j], hyb_dq_g[j], hyb_dk_g[j],
                                     hyb_lq1[j], hyb_lk1[j], hyb_lq2[j], hyb_lk2[j],
                                     hyb_subln_g[j], hyb_w_out[j], l)
        else:
            x = x + _fox_mixer(h, fox_w_in[j], fox_b_f[j], fox_q_g[j], fox_k_g[j], fox_w_out[j])
        h = _rmsnorm(x, norm2_g[l])
        x = x + _sqrelu_mlp(h, mlp_w1[l], mlp_w2[l])
    return x
```

```python
import functools
import math

import jax
import jax.numpy as jnp
from jax import lax
from jax.experimental import pallas as pl
from jax.experimental.pallas import tpu as pltpu

F32 = jnp.float32
BF16 = jnp.bfloat16
EPS = 1e-6
LOG2E = math.log2(math.e)
LANES = 128
BF16_SUBLANES = 16
MIB = 2 ** 20

CONV_WIDTH = 3
DIFF_HEADS = 8
DIFF_D = 64
FOX_HEADS = 16
FOX_D = 128

_NT = (((1,), (1,)), ((), ()))


def _cparams(semantics, vmem_mib):
    return pltpu.CompilerParams(dimension_semantics=semantics, vmem_limit_bytes=vmem_mib * MIB)


def _normalize_rows(x_ref, g_ref, h_ref):
    x = x_ref[...]
    ms = jnp.mean(x * x, axis=-1, keepdims=True)
    h_ref[...] = ((x * lax.rsqrt(ms + EPS)) * g_ref[...]).astype(BF16)


def _norm_mm_body(x_ref, g_ref, w_ref, o_ref, h_ref, *, sqrelu):
    @pl.when(pl.program_id(1) == 0)
    def _():
        _normalize_rows(x_ref, g_ref, h_ref)

    y = jnp.dot(h_ref[...], w_ref[...], preferred_element_type=F32)
    if sqrelu:
        y = jnp.square(jnp.maximum(y, 0.0))
    o_ref[...] = y.astype(o_ref.dtype)


def _norm_mm_gate_body(x_ref, g_ref, w_ref, wf_ref, o_ref, ft_ref, h_ref):
    @pl.when(pl.program_id(1) == 0)
    def _():
        _normalize_rows(x_ref, g_ref, h_ref)
        ft_ref[...] = lax.dot_general(wf_ref[...], h_ref[...], _NT, preferred_element_type=F32)

    y = jnp.dot(h_ref[...], w_ref[...], preferred_element_type=F32)
    o_ref[...] = y.astype(o_ref.dtype)


def _norm_mm(x, g, w, *, sqrelu=False, wf_t=None, tm=1024, tn=1024):
    n, d = x.shape
    nout = w.shape[1]
    tm, tn = min(tm, n), min(tn, nout)
    grid = (n // tm, nout // tn)
    x_spec = pl.BlockSpec((tm, d), lambda i, j: (i, 0))
    g_spec = pl.BlockSpec((1, d), lambda i, j: (0, 0))
    w_spec = pl.BlockSpec((d, tn), lambda i, j: (0, j))
    o_spec = pl.BlockSpec((tm, tn), lambda i, j: (i, j))
    scratch = [pltpu.VMEM((tm, d), BF16)]
    params = _cparams(("parallel", "arbitrary"), 48)
    g2 = g.reshape(1, d).astype(F32)
    if wf_t is None:
        return pl.pallas_call(
            functools.partial(_norm_mm_body, sqrelu=sqrelu),
            grid=grid, in_specs=[x_spec, g_spec, w_spec], out_specs=o_spec,
            out_shape=jax.ShapeDtypeStruct((n, nout), BF16),
            scratch_shapes=scratch, compiler_params=params, name="norm_mm")(x, g2, w)
    heads = wf_t.shape[0]
    return pl.pallas_call(
        _norm_mm_gate_body,
        grid=grid,
        in_specs=[x_spec, g_spec, w_spec, pl.BlockSpec((heads, d), lambda i, j: (0, 0))],
        out_specs=[o_spec, pl.BlockSpec((heads, tm), lambda i, j: (0, i))],
        out_shape=[jax.ShapeDtypeStruct((n, nout), BF16), jax.ShapeDtypeStruct((heads, n), F32)],
        scratch_shapes=scratch, compiler_params=params, name="norm_mm_gate")(x, g2, w, wf_t)


def _mm_res_body(a_ref, w_ref, r_ref, o_ref):
    d = jnp.dot(a_ref[...], w_ref[...], preferred_element_type=F32)

    @pl.when(pl.program_id(2) == 0)
    def _():
        o_ref[...] = r_ref[...] + d

    @pl.when(pl.program_id(2) != 0)
    def _():
        o_ref[...] += d


def _mm_res(a, w, r, *, tm=1024, tn=1024, tk=2048):
    n, kdim = a.shape
    nout = w.shape[1]
    tm, tn, tk = min(tm, n), min(tn, nout), min(tk, kdim)
    return pl.pallas_call(
        _mm_res_body,
        grid=(n // tm, nout // tn, kdim // tk),
        in_specs=[pl.BlockSpec((tm, tk), lambda i, j, k: (i, k)),
                  pl.BlockSpec((tk, tn), lambda i, j, k: (k, j)),
                  pl.BlockSpec((tm, tn), lambda i, j, k: (i, j))],
        out_specs=pl.BlockSpec((tm, tn), lambda i, j, k: (i, j)),
        out_shape=jax.ShapeDtypeStruct((n, nout), F32),
        compiler_params=_cparams(("parallel", "parallel", "arbitrary"), 48),
        name="mm_res")(a, w, r)


def _hyb_out_body(gb_ref, gc_ref, u_ref, gcp_ref, up_ref, att_ref, cw_ref, w_ref, r_ref,
                  out_ref, a_ref, *, tm, seq):
    c = gb_ref.shape[1]

    @pl.when(pl.program_id(1) == 0)
    def _():
        z = gc_ref[...].astype(F32) * u_ref[...].astype(F32)
        zp = gcp_ref[...].astype(F32) * up_ref[...].astype(F32)
        first_of_seq = lax.rem(pl.program_id(0) * tm, seq) == 0
        zp = jnp.where(first_of_seq, 0.0, zp)
        zext = jnp.concatenate([zp, z], axis=0)
        z1 = pltpu.roll(zext, 1, 0)[BF16_SUBLANES:]
        z2 = pltpu.roll(zext, 2, 0)[BF16_SUBLANES:]
        cw = cw_ref[...]
        y = gb_ref[...].astype(F32) * (cw[0:1] * z2 + cw[1:2] * z1 + cw[2:3] * z)
        a_ref[:, :c] = y.astype(BF16)
        a_ref[:, c:] = att_ref[...]

    out_ref[...] = r_ref[...] + jnp.dot(a_ref[...], w_ref[...], preferred_element_type=F32)


def _hyb_out(proj, att, conv_w, w, r, *, seq, tm=512, tn=1024):
    n = proj.shape[0]
    c = conv_w.shape[1]
    kdim, nout = w.shape
    tm, tn = min(tm, n, seq), min(tn, nout)
    halo = BF16_SUBLANES
    prev = lambda col: (lambda i, j: (jnp.maximum(i * (tm // halo) - 1, 0), col))
    return pl.pallas_call(
        functools.partial(_hyb_out_body, tm=tm, seq=seq),
        grid=(n // tm, nout // tn),
        in_specs=[pl.BlockSpec((tm, c), lambda i, j: (i, 0)),
                  pl.BlockSpec((tm, c), lambda i, j: (i, 1)),
                  pl.BlockSpec((tm, c), lambda i, j: (i, 2)),
                  pl.BlockSpec((halo, c), prev(1)),
                  pl.BlockSpec((halo, c), prev(2)),
                  pl.BlockSpec((tm, c), lambda i, j: (i, 0)),
                  pl.BlockSpec((CONV_WIDTH, c), lambda i, j: (0, 0)),
                  pl.BlockSpec((kdim, tn), lambda i, j: (0, j)),
                  pl.BlockSpec((tm, tn), lambda i, j: (i, j))],
        out_specs=pl.BlockSpec((tm, tn), lambda i, j: (i, j)),
        out_shape=jax.ShapeDtypeStruct((n, nout), F32),
        scratch_shapes=[pltpu.VMEM((tm, kdim), BF16)],
        compiler_params=_cparams(("parallel", "arbitrary"), 56),
        name="hyb_out")(proj, proj, proj, proj, proj, att, conv_w.astype(F32), w, r)


def _block_diag_ones(group):
    r = lax.broadcasted_iota(jnp.int32, (LANES, LANES), 0) // group
    c = lax.broadcasted_iota(jnp.int32, (LANES, LANES), 1) // group
    return (r == c).astype(BF16)


def _group_rmsnorm(x, gain, bd, group):
    xf = x.astype(F32)
    sq = xf * xf
    hi = sq.astype(BF16)
    lo = (sq - hi.astype(F32)).astype(BF16)
    gs = (jnp.dot(hi, bd, preferred_element_type=F32) + jnp.dot(lo, bd, preferred_element_type=F32))
    return (xf * lax.rsqrt(gs * (1.0 / group) + EPS)) * gain


def _qk_norm_body(q_ref, k_ref, gq_ref, gk_ref, qo_ref, ko_ref, *, group, split_k):
    bd = _block_diag_ones(group)
    gq, gk = gq_ref[...], gk_ref[...]
    low_half = lax.broadcasted_iota(jnp.int32, (1, LANES), 1) < group
    for t in range(q_ref.shape[1] // LANES):
        cols = slice(t * LANES, (t + 1) * LANES)
        qo_ref[:, cols] = _group_rmsnorm(q_ref[:, cols], gq, bd, group).astype(BF16)
        kn = _group_rmsnorm(k_ref[:, cols], gk, bd, group)
        if split_k:
            ko_ref[:, 2 * t * LANES:(2 * t + 1) * LANES] = jnp.where(low_half, kn, 0.0).astype(BF16)
            ko_ref[:, (2 * t + 1) * LANES:(2 * t + 2) * LANES] = jnp.where(low_half, 0.0, kn).astype(BF16)
        else:
            ko_ref[:, cols] = kn.astype(BF16)


def _qk_norm(proj, gq, gk, *, q_col, k_col, width, group, split_k, tm=1024, cw=1024):
    n = proj.shape[0]
    tm = min(tm, n)
    kmul = 2 if split_k else 1
    reps = LANES // gq.shape[0]
    gq2 = jnp.tile(gq.astype(F32), reps).reshape(1, LANES)
    gk2 = jnp.tile(gk.astype(F32), reps).reshape(1, LANES)
    return pl.pallas_call(
        functools.partial(_qk_norm_body, group=group, split_k=split_k),
        grid=(n // tm, width // cw),
        in_specs=[pl.BlockSpec((tm, cw), lambda i, j: (i, q_col // cw + j)),
                  pl.BlockSpec((tm, cw), lambda i, j: (i, k_col // cw + j)),
                  pl.BlockSpec((1, LANES), lambda i, j: (0, 0)),
                  pl.BlockSpec((1, LANES), lambda i, j: (0, 0))],
        out_specs=[pl.BlockSpec((tm, cw), lambda i, j: (i, j)),
                   pl.BlockSpec((tm, kmul * cw), lambda i, j: (i, j))],
        out_shape=[jax.ShapeDtypeStruct((n, width), BF16),
                   jax.ShapeDtypeStruct((n, kmul * width), BF16)],
        compiler_params=_cparams(("parallel", "parallel"), 40),
        name="qk_norm")(proj, proj, gq2, gk2)


def _gate_cumsum_body(f_ref, b_ref, c_ref, tri_ref, carry_ref, *, tc):
    @pl.when(pl.program_id(1) == 0)
    def _():
        carry_ref[...] = jnp.zeros_like(carry_ref)
        r = lax.broadcasted_iota(jnp.int32, (tc, tc), 0)
        c = lax.broadcasted_iota(jnp.int32, (tc, tc), 1)
        tri_ref[...] = (r <= c).astype(BF16)

    f = f_ref[...] + b_ref[...]
    logf = -(jnp.maximum(-f, 0.0) + jnp.log1p(jnp.exp(-jnp.abs(f))))
    hi = logf.astype(BF16)
    r1 = logf - hi.astype(F32)
    mid = r1.astype(BF16)
    lo = (r1 - mid.astype(F32)).astype(BF16)
    tri = tri_ref[...]
    c = (jnp.dot(hi, tri, preferred_element_type=F32) + jnp.dot(mid, tri, preferred_element_type=F32)
         + jnp.dot(lo, tri, preferred_element_type=F32)) + carry_ref[:, 0:1]
    carry_ref[...] = jnp.broadcast_to(c[:, tc - 1:tc], carry_ref.shape)
    c_ref[...] = c * LOG2E


def _gate_cumsum(f_t, b_f, *, batch, seq, tc=512):
    heads = f_t.shape[0]
    tc = min(tc, seq)
    nt = seq // tc
    return pl.pallas_call(
        functools.partial(_gate_cumsum_body, tc=tc),
        grid=(batch, nt),
        in_specs=[pl.BlockSpec((heads, tc), lambda b, j: (0, b * nt + j)),
                  pl.BlockSpec((heads, 1), lambda b, j: (0, 0))],
        out_specs=pl.BlockSpec((heads, tc), lambda b, j: (0, b * nt + j)),
        out_shape=jax.ShapeDtypeStruct(f_t.shape, F32),
        scratch_shapes=[pltpu.VMEM((tc, tc), BF16), pltpu.VMEM((heads, LANES), F32)],
        compiler_params=_cparams(("arbitrary", "arbitrary"), 32),
        name="gate_cumsum")(f_t, b_f.reshape(heads, 1).astype(F32))


def _causal_keep(tq, tk, col0):
    row = lax.broadcasted_iota(jnp.int32, (tq, tk), 0)
    col = lax.broadcasted_iota(jnp.int32, (tq, tk), 1) + col0
    return col <= row


def _online_softmax_step(s, v, m, l, acc):
    m_new = jnp.maximum(m, jnp.max(s, axis=1, keepdims=True))
    alpha = jnp.exp2(m - m_new)
    p = jnp.exp2(s - m_new)
    l = alpha * l + jnp.sum(p, axis=1, keepdims=True)
    acc = alpha * acc + jnp.dot(p.astype(BF16), v, preferred_element_type=F32)
    return m_new, l, acc


def _fox_attn_body(q_ref, k_ref, v_ref, ck_ref, o_ref, *, tq, tk):
    qi = pl.program_id(2)
    q = q_ref[0]
    dv = v_ref.shape[2]
    sub = tq // tk

    def chunk(j, carry, masked_col0):
        start = pl.multiple_of(j * tk, tk)
        k = k_ref[0, pl.ds(start, tk), :]
        v = v_ref[0, pl.ds(start, tk), :]
        s = lax.dot_general(q, k, _NT, preferred_element_type=F32) - ck_ref[0, 0, j]
        if masked_col0 is not None:
            s = jnp.where(_causal_keep(tq, tk, masked_col0), s, -jnp.inf)
        return _online_softmax_step(s, v, *carry)

    carry = (jnp.full((tq, 1), -jnp.inf, F32), jnp.zeros((tq, 1), F32), jnp.zeros((tq, dv), F32))
    carry = lax.fori_loop(0, qi * sub, lambda j, c: chunk(j, c, None), carry)
    for d in range(sub):
        carry = chunk(qi * sub + d, carry, d * tk)
    _, l, acc = carry
    o_ref[0] = (acc / l).astype(o_ref.dtype)


def _fox_attn(qn, kn, proj, c_t, *, v_col, tq=512, tk=512):
    b, t, _ = qn.shape
    tq, tk = min(tq, t), min(tk, t)
    ck = c_t.reshape(FOX_HEADS, b, t // tk, 1, tk)
    vb = v_col // FOX_D
    return pl.pallas_call(
        functools.partial(_fox_attn_body, tq=tq, tk=tk),
        grid=(b, FOX_HEADS, t // tq),
        in_specs=[pl.BlockSpec((1, tq, FOX_D), lambda bi, h, i: (bi, i, h)),
                  pl.BlockSpec((1, t, FOX_D), lambda bi, h, i: (bi, 0, h)),
                  pl.BlockSpec((1, t, FOX_D), lambda bi, h, i: (bi, 0, vb + h)),
                  pl.BlockSpec((1, 1, t // tk, 1, tk), lambda bi, h, i: (h, bi, 0, 0, 0))],
        out_specs=pl.BlockSpec((1, tq, FOX_D), lambda bi, h, i: (bi, i, h)),
        out_shape=jax.ShapeDtypeStruct((b, t, FOX_HEADS * FOX_D), BF16),
        compiler_params=_cparams(("parallel", "parallel", "arbitrary"), 40),
        name="fox_attn")(qn, kn, proj, ck)


def _diff_attn_body(lq1_ref, lk1_ref, lq2_ref, lk2_ref, sg_ref, q_ref, k_ref, v_ref, o_ref,
                    *, tq, tk, lam_init):
    qi = pl.program_id(2)
    q = q_ref[0]
    dv = v_ref.shape[2]
    sub = tq // tk

    def chunk(j, carry, masked_col0):
        start = pl.multiple_of(j * tk, tk)
        v = v_ref[0, pl.ds(start, tk), :]
        keep = None if masked_col0 is None else _causal_keep(tq, tk, masked_col0)
        out = []
        for half in range(2):
            k = k_ref[0, pl.ds(start, tk), half * LANES:(half + 1) * LANES]
            s = lax.dot_general(q, k, _NT, preferred_element_type=F32)
            if keep is not None:
                s = jnp.where(keep, s, -jnp.inf)
            out.extend(_online_softmax_step(s, v, *carry[3 * half:3 * half + 3]))
        return tuple(out)

    one = (jnp.full((tq, 1), -jnp.inf, F32), jnp.zeros((tq, 1), F32), jnp.zeros((tq, dv), F32))
    carry = lax.fori_loop(0, qi * sub, lambda j, c: chunk(j, c, None), one + one)
    for d in range(sub):
        carry = chunk(qi * sub + d, carry, d * tk)
    _, l0, a0, _, l1, a1 = carry

    lam = (jnp.exp(jnp.sum(lq1_ref[...] * lk1_ref[...], keepdims=True))
           - jnp.exp(jnp.sum(lq2_ref[...] * lk2_ref[...], keepdims=True)) + lam_init)
    o = a0 / l0 - lam * (a1 / l1)
    ms = jnp.mean(o * o, axis=-1, keepdims=True)
    o = ((o * lax.rsqrt(ms + EPS)) * sg_ref[...]) * (1.0 - lam_init)
    o_ref[0] = o.astype(o_ref.dtype)


def _diff_attn(qn, kn, proj, lq1, lk1, lq2, lk2, subln_g, *, v_col, lam_init, tq=512, tk=512):
    b, t, _ = qn.shape
    tq, tk = min(tq, t), min(tk, t)
    dv = 2 * DIFF_D
    vb = v_col // dv
    vec = lambda a: a.reshape(1, -1).astype(F32)
    small = lambda w: pl.BlockSpec((1, w), lambda bi, h, i: (0, 0))
    return pl.pallas_call(
        functools.partial(_diff_attn_body, tq=tq, tk=tk, lam_init=lam_init),
        grid=(b, DIFF_HEADS, t // tq),
        in_specs=[small(DIFF_D), small(DIFF_D), small(DIFF_D), small(DIFF_D), small(dv),
                  pl.BlockSpec((1, tq, dv), lambda bi, h, i: (bi, i, h)),
                  pl.BlockSpec((1, t, 2 * dv), lambda bi, h, i: (bi, 0, h)),
                  pl.BlockSpec((1, t, dv), lambda bi, h, i: (bi, 0, vb + h))],
        out_specs=pl.BlockSpec((1, tq, dv), lambda bi, h, i: (bi, i, h)),
        out_shape=jax.ShapeDtypeStruct((b, t, DIFF_HEADS * dv), BF16),
        compiler_params=_cparams(("parallel", "parallel", "arbitrary"), 40),
        name="diff_attn")(vec(lq1), vec(lk1), vec(lq2), vec(lk2), vec(subln_g), qn, kn, proj)


def kernel(x, norm1_g, norm2_g, hyb_w_in, hyb_conv_w, hyb_dq_g, hyb_dk_g, hyb_lq1, hyb_lk1,
           hyb_lq2, hyb_lk2, hyb_subln_g, hyb_w_out, fox_w_in, fox_b_f, fox_q_g, fox_k_g,
           fox_w_out, mlp_w1, mlp_w2):
    b, t, d = x.shape
    n = b * t
    depth = norm1_g.shape[0]
    conv_dim = hyb_conv_w.shape[2]
    diff_qk = DIFF_HEADS * 2 * DIFF_D
    xf = x.reshape(n, d)
    for l in range(depth):
        j = l // 2
        if l % 2 == 0:
            proj = _norm_mm(xf, norm1_g[l], hyb_w_in[j].astype(BF16))
            q_col = 3 * conv_dim
            k_col = q_col + diff_qk
            v_col = k_col + diff_qk
            gq = hyb_dq_g[j].astype(F32) * (DIFF_D ** -0.5 * LOG2E)
            qn, kn = _qk_norm(proj, gq, hyb_dk_g[j], q_col=q_col, k_col=k_col, width=diff_qk,
                              group=DIFF_D, split_k=True)
            att = _diff_attn(qn.reshape(b, t, -1), kn.reshape(b, t, -1), proj.reshape(b, t, -1),
                             hyb_lq1[j], hyb_lk1[j], hyb_lq2[j], hyb_lk2[j], hyb_subln_g[j],
                             v_col=v_col, lam_init=0.8 - 0.6 * math.exp(-0.3 * l))
            xf = _hyb_out(proj, att.reshape(n, -1), hyb_conv_w[j], hyb_w_out[j].astype(BF16), xf,
                          seq=t)
        else:
            w_in = fox_w_in[j]
            wf_t = w_in[:, 3 * d:].T.astype(BF16)
            proj, f_t = _norm_mm(xf, norm1_g[l], w_in[:, :3 * d].astype(BF16), wf_t=wf_t)
            c_t = _gate_cumsum(f_t, fox_b_f[j], batch=b, seq=t)
            gq = fox_q_g[j].astype(F32) * (FOX_D ** -0.5 * LOG2E)
            qn, kn = _qk_norm(proj, gq, fox_k_g[j], q_col=0, k_col=d, width=d, group=FOX_D,
                              split_k=False)
            att = _fox_attn(qn.reshape(b, t, -1), kn.reshape(b, t, -1), proj.reshape(b, t, -1),
                            c_t, v_col=2 * d)
            xf = _mm_res(att.reshape(n, -1), fox_w_out[j].astype(BF16), xf)
        hid = _norm_mm(xf, norm2_g[l], mlp_w1[l].astype(BF16), sqrelu=True)
        xf = _mm_res(hid, mlp_w2[l].astype(BF16), xf)
    return xf.reshape(b, t, d)
```

```python
import functools
import math

import jax
import jax.numpy as jnp
from jax import lax
from jax.experimental import pallas as pl
from jax.experimental.pallas import tpu as pltpu

F32 = jnp.float32
BF16 = jnp.bfloat16
EPS = 1e-6
LOG2E = math.log2(math.e)
LANES = 128
BF16_SUBLANES = 16
MIB = 2 ** 20

CONV_WIDTH = 3
DIFF_HEADS = 8
DIFF_D = 64
FOX_HEADS = 16
FOX_D = 128

_NT = (((1,), (1,)), ((), ()))


def _cparams(semantics, vmem_mib):
    return pltpu.CompilerParams(dimension_semantics=semantics, vmem_limit_bytes=vmem_mib * MIB)


def _split3(x):
    hi = x.astype(BF16)
    r1 = x - hi.astype(F32)
    mid = r1.astype(BF16)
    lo = (r1 - mid.astype(F32)).astype(BF16)
    return hi, mid, lo


def _normalize_rows(x_ref, g_ref, h_ref):
    x = x_ref[...]
    ms = jnp.mean(x * x, axis=-1, keepdims=True)
    h_ref[...] = ((x * lax.rsqrt(ms + EPS)) * g_ref[...]).astype(BF16)


def _norm_mm_body(x_ref, g_ref, w_ref, o_ref, h_ref, *, sqrelu):
    @pl.when(pl.program_id(1) == 0)
    def _():
        _normalize_rows(x_ref, g_ref, h_ref)

    y = jnp.dot(h_ref[...], w_ref[...], preferred_element_type=F32)
    if sqrelu:
        y = jnp.square(jnp.maximum(y, 0.0))
    o_ref[...] = y.astype(o_ref.dtype)


def _norm_mm_gate_body(x_ref, g_ref, w_ref, wf_ref, o_ref, f_ref, h_ref):
    @pl.when(pl.program_id(1) == 0)
    def _():
        _normalize_rows(x_ref, g_ref, h_ref)
        f_ref[...] = jnp.dot(h_ref[...], wf_ref[...], preferred_element_type=F32)

    y = jnp.dot(h_ref[...], w_ref[...], preferred_element_type=F32)
    o_ref[...] = y.astype(o_ref.dtype)


def _norm_mm(x, g, w, *, sqrelu=False, wf=None, tm=1024, tn=1024):
    n, d = x.shape
    nout = w.shape[1]
    tm, tn = min(tm, n), min(tn, nout)
    grid = (n // tm, nout // tn)
    x_spec = pl.BlockSpec((tm, d), lambda i, j: (i, 0))
    g_spec = pl.BlockSpec((1, d), lambda i, j: (0, 0))
    w_spec = pl.BlockSpec((d, tn), lambda i, j: (0, j))
    o_spec = pl.BlockSpec((tm, tn), lambda i, j: (i, j))
    scratch = [pltpu.VMEM((tm, d), BF16)]
    params = _cparams(("parallel", "arbitrary"), 48)
    g2 = g.reshape(1, d).astype(F32)
    if wf is None:
        return pl.pallas_call(
            functools.partial(_norm_mm_body, sqrelu=sqrelu),
            grid=grid, in_specs=[x_spec, g_spec, w_spec], out_specs=o_spec,
            out_shape=jax.ShapeDtypeStruct((n, nout), BF16),
            scratch_shapes=scratch, compiler_params=params, name="norm_mm")(x, g2, w)
    nf = wf.shape[1]
    return pl.pallas_call(
        _norm_mm_gate_body,
        grid=grid,
        in_specs=[x_spec, g_spec, w_spec, pl.BlockSpec((d, nf), lambda i, j: (0, 0))],
        out_specs=[o_spec, pl.BlockSpec((tm, nf), lambda i, j: (i, 0))],
        out_shape=[jax.ShapeDtypeStruct((n, nout), BF16), jax.ShapeDtypeStruct((n, nf), F32)],
        scratch_shapes=scratch, compiler_params=params, name="norm_mm_gate")(x, g2, w, wf)


def _mm_res_body(a_ref, w_ref, r_ref, o_ref):
    d = jnp.dot(a_ref[...], w_ref[...], preferred_element_type=F32)

    @pl.when(pl.program_id(2) == 0)
    def _():
        o_ref[...] = r_ref[...] + d

    @pl.when(pl.program_id(2) != 0)
    def _():
        o_ref[...] += d


def _mm_res(a, w, r, *, tm=1024, tn=1024, tk=2048):
    n, kdim = a.shape
    nout = w.shape[1]
    tm, tn, tk = min(tm, n), min(tn, nout), min(tk, kdim)
    return pl.pallas_call(
        _mm_res_body,
        grid=(n // tm, nout // tn, kdim // tk),
        in_specs=[pl.BlockSpec((tm, tk), lambda i, j, k: (i, k)),
                  pl.BlockSpec((tk, tn), lambda i, j, k: (k, j)),
                  pl.BlockSpec((tm, tn), lambda i, j, k: (i, j))],
        out_specs=pl.BlockSpec((tm, tn), lambda i, j, k: (i, j)),
        out_shape=jax.ShapeDtypeStruct((n, nout), F32),
        compiler_params=_cparams(("parallel", "parallel", "arbitrary"), 48),
        name="mm_res")(a, w, r)


def _mm2_res_body(a1_ref, a2_ref, w_ref, r_ref, o_ref):
    k1 = a1_ref.shape[1]
    o_ref[...] = (r_ref[...]
                  + jnp.dot(a1_ref[...], w_ref[:k1, :], preferred_element_type=F32)
                  + jnp.dot(a2_ref[...], w_ref[k1:, :], preferred_element_type=F32))


def _mm2_res(a1, a2, w, r, *, a2_block=0, tm=512):
    n = a1.shape[0]
    kdim, nout = w.shape
    k1 = k2 = kdim // 2
    tm = min(tm, n)
    return pl.pallas_call(
        _mm2_res_body,
        grid=(n // tm,),
        in_specs=[pl.BlockSpec((tm, k1), lambda i: (i, 0)),
                  pl.BlockSpec((tm, k2), lambda i: (i, a2_block)),
                  pl.BlockSpec((k1 + k2, nout), lambda i: (0, 0)),
                  pl.BlockSpec((tm, nout), lambda i: (i, 0))],
        out_specs=pl.BlockSpec((tm, nout), lambda i: (i, 0)),
        out_shape=jax.ShapeDtypeStruct((n, nout), F32),
        compiler_params=_cparams(("parallel",), 48),
        name="mm2_res")(a1, a2, w, r)


def _block_diag_ones(group):
    r = lax.broadcasted_iota(jnp.int32, (LANES, LANES), 0) // group
    c = lax.broadcasted_iota(jnp.int32, (LANES, LANES), 1) // group
    return (r == c).astype(BF16)


def _group_rmsnorm(x, gain, bd, group):
    xf = x.astype(F32)
    sq = xf * xf
    hi = sq.astype(BF16)
    lo = (sq - hi.astype(F32)).astype(BF16)
    gs = (jnp.dot(hi, bd, preferred_element_type=F32) + jnp.dot(lo, bd, preferred_element_type=F32))
    return (xf * lax.rsqrt(gs * (1.0 / group) + EPS)) * gain


def _hyb_prep_body(gb_ref, gc_ref, u_ref, gcp_ref, up_ref, q_ref, k_ref, cw_ref, gq_ref, gk_ref,
                   y_ref, qo_ref, ko_ref, *, tm, seq):
    z = gc_ref[...].astype(F32) * u_ref[...].astype(F32)
    zp = gcp_ref[...].astype(F32) * up_ref[...].astype(F32)
    first_of_seq = lax.rem(pl.program_id(0) * tm, seq) == 0
    zp = jnp.where(first_of_seq, 0.0, zp)
    zext = jnp.concatenate([zp, z], axis=0)
    z1 = pltpu.roll(zext, 1, 0)[BF16_SUBLANES:]
    z2 = pltpu.roll(zext, 2, 0)[BF16_SUBLANES:]
    cw = cw_ref[...]
    y = gb_ref[...].astype(F32) * (cw[0:1] * z2 + cw[1:2] * z1 + cw[2:3] * z)
    y_ref[...] = y.astype(BF16)

    bd = _block_diag_ones(DIFF_D)
    gq, gk = gq_ref[...], gk_ref[...]
    low_half = lax.broadcasted_iota(jnp.int32, (1, LANES), 1) < DIFF_D
    for t in range(q_ref.shape[1] // LANES):
        cols = slice(t * LANES, (t + 1) * LANES)
        qo_ref[:, cols] = _group_rmsnorm(q_ref[:, cols], gq, bd, DIFF_D).astype(BF16)
        kn = _group_rmsnorm(k_ref[:, cols], gk, bd, DIFF_D)
        ko_ref[:, 2 * t * LANES:(2 * t + 1) * LANES] = jnp.where(low_half, kn, 0.0).astype(BF16)
        ko_ref[:, (2 * t + 1) * LANES:(2 * t + 2) * LANES] = jnp.where(low_half, 0.0, kn).astype(BF16)


def _hyb_prep(proj, conv_w, gq, gk, *, seq, tm=512):
    n = proj.shape[0]
    c = conv_w.shape[1]
    tm = min(tm, n, seq)
    halo = BF16_SUBLANES
    prev = lambda col: (lambda i: (jnp.maximum(i * (tm // halo) - 1, 0), col))
    col = lambda b: pl.BlockSpec((tm, c), lambda i: (i, b))
    vec = lambda a: jnp.tile(a.astype(F32), LANES // a.shape[0]).reshape(1, LANES)
    lane_vec = pl.BlockSpec((1, LANES), lambda i: (0, 0))
    return pl.pallas_call(
        functools.partial(_hyb_prep_body, tm=tm, seq=seq),
        grid=(n // tm,),
        in_specs=[col(0), col(1), col(2),
                  pl.BlockSpec((halo, c), prev(1)), pl.BlockSpec((halo, c), prev(2)),
                  col(3), col(4),
                  pl.BlockSpec((CONV_WIDTH, c), lambda i: (0, 0)), lane_vec, lane_vec],
        out_specs=[pl.BlockSpec((tm, c), lambda i: (i, 0)),
                   pl.BlockSpec((tm, c), lambda i: (i, 0)),
                   pl.BlockSpec((tm, 2 * c), lambda i: (i, 0))],
        out_shape=[jax.ShapeDtypeStruct((n, c), BF16), jax.ShapeDtypeStruct((n, c), BF16),
                   jax.ShapeDtypeStruct((n, 2 * c), BF16)],
        compiler_params=_cparams(("parallel",), 48),
        name="hyb_prep")(proj, proj, proj, proj, proj, proj, proj, conv_w.astype(F32), vec(gq), vec(gk))


def _fox_prep_body(q_ref, k_ref, c_ref, gq_ref, gk_ref, qo_ref, ko_ref):
    bd = _block_diag_ones(FOX_D)
    gq, gk = gq_ref[...], gk_ref[...]
    lane = lax.broadcasted_iota(jnp.int32, (1, LANES), 1)
    ones_q = jnp.where((lane >= 3) & (lane < 6), 1.0, 0.0)
    ones_k = jnp.where(lane < 3, 1.0, 0.0)
    c_hi, c_mid, c_lo = (t.astype(F32) for t in _split3(c_ref[...]))
    for h in range(q_ref.shape[1] // LANES):
        cols = slice(h * LANES, (h + 1) * LANES)
        qo_ref[:, 2 * h * LANES:(2 * h + 1) * LANES] = _group_rmsnorm(q_ref[:, cols], gq, bd, FOX_D).astype(BF16)
        ko_ref[:, 2 * h * LANES:(2 * h + 1) * LANES] = _group_rmsnorm(k_ref[:, cols], gk, bd, FOX_D).astype(BF16)
        hi, mid, lo = c_hi[:, h:h + 1], c_mid[:, h:h + 1], c_lo[:, h:h + 1]
        split_q = jnp.where(lane == 0, hi, jnp.where(lane == 1, mid, jnp.where(lane == 2, lo, 0.0)))
        split_k = jnp.where(lane == 3, hi, jnp.where(lane == 4, mid, jnp.where(lane == 5, lo, 0.0)))
        qo_ref[:, (2 * h + 1) * LANES:(2 * h + 2) * LANES] = (split_q + ones_q).astype(BF16)
        ko_ref[:, (2 * h + 1) * LANES:(2 * h + 2) * LANES] = (ones_k - split_k).astype(BF16)


def _fox_prep(proj, c, gq, gk, *, width, tm=512):
    n = proj.shape[0]
    tm = min(tm, n)
    vec = lambda a: a.astype(F32).reshape(1, LANES)
    lane_vec = pl.BlockSpec((1, LANES), lambda i: (0, 0))
    return pl.pallas_call(
        _fox_prep_body,
        grid=(n // tm,),
        in_specs=[pl.BlockSpec((tm, width), lambda i: (i, 0)),
                  pl.BlockSpec((tm, width), lambda i: (i, 1)),
                  pl.BlockSpec((tm, LANES), lambda i: (i, 0)), lane_vec, lane_vec],
        out_specs=[pl.BlockSpec((tm, 2 * width), lambda i: (i, 0)),
                   pl.BlockSpec((tm, 2 * width), lambda i: (i, 0))],
        out_shape=[jax.ShapeDtypeStruct((n, 2 * width), BF16),
                   jax.ShapeDtypeStruct((n, 2 * width), BF16)],
        compiler_params=_cparams(("parallel",), 48),
        name="fox_prep")(proj, proj, c, vec(gq), vec(gk))


def _gate_cumsum_body(f_ref, b_ref, c_ref, tri_ref, carry_ref, *, tc):
    @pl.when(pl.program_id(1) == 0)
    def _():
        carry_ref[...] = jnp.zeros_like(carry_ref)
        r = lax.broadcasted_iota(jnp.int32, (tc, tc), 0)
        c = lax.broadcasted_iota(jnp.int32, (tc, tc), 1)
        tri_ref[...] = (r >= c).astype(BF16)

    f = f_ref[...] + b_ref[...]
    logf = -(jnp.maximum(-f, 0.0) + jnp.log1p(jnp.exp(-jnp.abs(f))))
    hi, mid, lo = _split3(logf)
    tri = tri_ref[...]
    c = (jnp.dot(tri, hi, preferred_element_type=F32) + jnp.dot(tri, mid, preferred_element_type=F32)
         + jnp.dot(tri, lo, preferred_element_type=F32)) + carry_ref[0:1, :]
    carry_ref[...] = jnp.broadcast_to(c[tc - 1:tc, :], carry_ref.shape)
    c_ref[...] = c * LOG2E


def _gate_cumsum(f, b_f, *, batch, seq, tc=512):
    nf = f.shape[1]
    tc = min(tc, seq)
    nt = seq // tc
    b2 = jnp.zeros((1, nf), F32).at[0, :b_f.shape[0]].set(b_f.astype(F32))
    return pl.pallas_call(
        functools.partial(_gate_cumsum_body, tc=tc),
        grid=(batch, nt),
        in_specs=[pl.BlockSpec((tc, nf), lambda b, j: (b * nt + j, 0)),
                  pl.BlockSpec((1, nf), lambda b, j: (0, 0))],
        out_specs=pl.BlockSpec((tc, nf), lambda b, j: (b * nt + j, 0)),
        out_shape=jax.ShapeDtypeStruct(f.shape, F32),
        scratch_shapes=[pltpu.VMEM((tc, tc), BF16), pltpu.VMEM((8, nf), F32)],
        compiler_params=_cparams(("arbitrary", "arbitrary"), 32),
        name="gate_cumsum")(f, b2)


def _flash_core(q_ref, k_ref, v_ref, s_ref, p_ref, a_ref, m_ref, l_ref, acc_ref, *, tk, nm, dk):
    qi = pl.program_id(2)
    upper, lower = slice(0, tk), slice(tk, 2 * tk)

    def kv_rows(c):
        if isinstance(c, int):
            return slice(c * tk, (c + 1) * tk)
        return pl.ds(pl.multiple_of(c * tk, tk), tk)

    def k_chunk(c, m):
        return k_ref[0, kv_rows(c), m * dk:(m + 1) * dk]

    def rep(x):
        return jnp.broadcast_to(x, (x.shape[0], LANES))

    def wide(x):
        return jnp.concatenate([x] * (tk // LANES), axis=1)

    def update(s, v, m_old, l_old, acc_old):
        mx = rep(jnp.max(s, axis=1, keepdims=True))
        m_new = mx if m_old is None else jnp.maximum(m_old, mx)
        p = jnp.exp2(s - wide(m_new))
        psum = rep(jnp.sum(p, axis=1, keepdims=True))
        pv = None if v is None else jnp.dot(p.astype(BF16), v, preferred_element_type=F32)
        if m_old is None:
            return m_new, psum, pv, None, p
        alpha = jnp.exp2(m_old - m_new)
        return m_new, alpha * l_old + psum, (None if v is None else alpha * acc_old + pv), alpha, p

    def diag(rows, c, masked, first):
        v = v_ref[0, kv_rows(c), :]
        for m in range(nm):
            s = lax.dot_general(q_ref[0, rows, :], k_chunk(c, m), _NT, preferred_element_type=F32)
            if masked:
                row = lax.broadcasted_iota(jnp.int32, (tk, tk), 0)
                col = lax.broadcasted_iota(jnp.int32, (tk, tk), 1)
                s = jnp.where(col <= row, s, -jnp.inf)
            if first:
                m_new, l_new, acc_new, _, _ = update(s, v, None, None, None)
            else:
                m_new, l_new, acc_new, _, _ = update(s, v, m_ref[m, rows], l_ref[m, rows],
                                                     acc_ref[m, rows])
            m_ref[m, rows] = m_new
            l_ref[m, rows] = l_new
            acc_ref[m, rows] = acc_new

    def scores(c, slot):
        for m in range(nm):
            s_ref[slot * nm + m] = lax.dot_general(q_ref[0], k_chunk(c, m), _NT,
                                                   preferred_element_type=F32)

    def softmax(slot):
        for m in range(nm):
            m_new, l_new, _, alpha, p = update(s_ref[slot * nm + m], None, m_ref[m], l_ref[m], None)
            m_ref[m] = m_new
            l_ref[m] = l_new
            a_ref[slot * nm + m] = alpha
            p_ref[slot * nm + m] = p.astype(BF16)

    def weighted_values(c, slot):
        v = v_ref[0, kv_rows(c), :]
        for m in range(nm):
            acc_ref[m] = a_ref[slot * nm + m] * acc_ref[m] + jnp.dot(
                p_ref[slot * nm + m], v, preferred_element_type=F32)

    diag(upper, 2 * qi, masked=True, first=True)
    diag(lower, 2 * qi, masked=False, first=True)
    diag(lower, 2 * qi + 1, masked=True, first=False)

    @pl.when(qi > 0)
    def _():
        n_vis = 2 * qi
        scores(0, 0)
        scores(1, 1)
        softmax(0)

        def pair(u, carry):
            c = 2 * u + 1
            scores(c + 1, 0)
            softmax(1)
            weighted_values(c - 1, 0)
            scores(c + 2, 1)
            softmax(0)
            weighted_values(c, 1)
            return carry

        lax.fori_loop(0, qi - 1, pair, 0)
        softmax(1)
        weighted_values(n_vis - 2, 0)
        weighted_values(n_vis - 1, 1)


def _flash_scratch(tq, tk, nm, dv):
    assert dv == LANES
    return [pltpu.VMEM((2 * nm, tq, tk), F32),
            pltpu.VMEM((2 * nm, tq, tk), BF16),
            pltpu.VMEM((2 * nm, tq, LANES), F32),
            pltpu.VMEM((nm, tq, LANES), F32),
            pltpu.VMEM((nm, tq, LANES), F32),
            pltpu.VMEM((nm, tq, dv), F32)]


def _fox_attn_body(q_ref, k_ref, v_ref, o_ref, *scratch, tk):
    _flash_core(q_ref, k_ref, v_ref, *scratch, tk=tk, nm=1, dk=k_ref.shape[2])
    l_ref, acc_ref = scratch[4], scratch[5]
    o_ref[0] = (acc_ref[0] / l_ref[0]).astype(o_ref.dtype)


def _fox_attn(qa, ka, proj, *, v_col, tk=512):
    b, t, _ = qa.shape
    tk = min(tk, t // 2)
    tq = 2 * tk
    dk = 2 * FOX_D
    vb = v_col // FOX_D
    return pl.pallas_call(
        functools.partial(_fox_attn_body, tk=tk),
        grid=(b, FOX_HEADS, t // tq),
        in_specs=[pl.BlockSpec((1, tq, dk), lambda bi, h, i: (bi, i, h)),
                  pl.BlockSpec((1, t, dk), lambda bi, h, i: (bi, 0, h)),
                  pl.BlockSpec((1, t, FOX_D), lambda bi, h, i: (bi, 0, vb + h))],
        out_specs=pl.BlockSpec((1, tq, FOX_D), lambda bi, h, i: (bi, i, h)),
        out_shape=jax.ShapeDtypeStruct((b, t, FOX_HEADS * FOX_D), BF16),
        scratch_shapes=_flash_scratch(tq, tk, 1, FOX_D),
        compiler_params=_cparams(("parallel", "parallel", "arbitrary"), 48),
        name="fox_attn")(qa, ka, proj)


def _diff_attn_body(lq1_ref, lk1_ref, lq2_ref, lk2_ref, sg_ref, q_ref, k_ref, v_ref, o_ref,
                    *scratch, tk, lam_init):
    _flash_core(q_ref, k_ref, v_ref, *scratch, tk=tk, nm=2, dk=q_ref.shape[2])
    l_ref, acc_ref = scratch[4], scratch[5]
    lam = (jnp.exp(jnp.sum(lq1_ref[...] * lk1_ref[...], keepdims=True))
           - jnp.exp(jnp.sum(lq2_ref[...] * lk2_ref[...], keepdims=True)) + lam_init)
    o = acc_ref[0] / l_ref[0] - lam * (acc_ref[1] / l_ref[1])
    ms = jnp.mean(o * o, axis=-1, keepdims=True)
    o = ((o * lax.rsqrt(ms + EPS)) * sg_ref[...]) * (1.0 - lam_init)
    o_ref[0] = o.astype(o_ref.dtype)


def _diff_attn(qn, kn, proj, lq1, lk1, lq2, lk2, subln_g, *, v_col, lam_init, tk=256):
    b, t, _ = qn.shape
    tk = min(tk, t // 2)
    tq = 2 * tk
    dv = 2 * DIFF_D
    vb = v_col // dv
    vec = lambda a: a.reshape(1, -1).astype(F32)
    small = lambda w: pl.BlockSpec((1, w), lambda bi, h, i: (0, 0))
    return pl.pallas_call(
        functools.partial(_diff_attn_body, tk=tk, lam_init=lam_init),
        grid=(b, DIFF_HEADS, t // tq),
        in_specs=[small(DIFF_D), small(DIFF_D), small(DIFF_D), small(DIFF_D), small(dv),
                  pl.BlockSpec((1, tq, dv), lambda bi, h, i: (bi, i, h)),
                  pl.BlockSpec((1, t, 2 * dv), lambda bi, h, i: (bi, 0, h)),
                  pl.BlockSpec((1, t, dv), lambda bi, h, i: (bi, 0, vb + h))],
        out_specs=pl.BlockSpec((1, tq, dv), lambda bi, h, i: (bi, i, h)),
        out_shape=jax.ShapeDtypeStruct((b, t, DIFF_HEADS * dv), BF16),
        scratch_shapes=_flash_scratch(tq, tk, 2, dv),
        compiler_params=_cparams(("parallel", "parallel", "arbitrary"), 48),
        name="diff_attn")(vec(lq1), vec(lk1), vec(lq2), vec(lk2), vec(subln_g), qn, kn, proj)


def kernel(x, norm1_g, norm2_g, hyb_w_in, hyb_conv_w, hyb_dq_g, hyb_dk_g, hyb_lq1, hyb_lk1,
           hyb_lq2, hyb_lk2, hyb_subln_g, hyb_w_out, fox_w_in, fox_b_f, fox_q_g, fox_k_g,
           fox_w_out, mlp_w1, mlp_w2):
    b, t, d = x.shape
    n = b * t
    depth = norm1_g.shape[0]
    conv_dim = hyb_conv_w.shape[2]
    xf = x.reshape(n, d)
    for l in range(depth):
        j = l // 2
        if l % 2 == 0:
            proj = _norm_mm(xf, norm1_g[l], hyb_w_in[j].astype(BF16))
            gq = hyb_dq_g[j].astype(F32) * (DIFF_D ** -0.5 * LOG2E)
            y_conv, qn, kn = _hyb_prep(proj, hyb_conv_w[j], gq, hyb_dk_g[j], seq=t)
            att = _diff_attn(qn.reshape(b, t, -1), kn.reshape(b, t, -1), proj.reshape(b, t, -1),
                             hyb_lq1[j], hyb_lk1[j], hyb_lq2[j], hyb_lk2[j], hyb_subln_g[j],
                             v_col=5 * conv_dim, lam_init=0.8 - 0.6 * math.exp(-0.3 * l))
            xf = _mm2_res(y_conv, att.reshape(n, -1), hyb_w_out[j].astype(BF16), xf)
        else:
            w_in = fox_w_in[j]
            wf = jnp.zeros((d, LANES), BF16).at[:, :FOX_HEADS].set(w_in[:, 3 * d:].astype(BF16))
            proj, f = _norm_mm(xf, norm1_g[l], w_in[:, :3 * d].astype(BF16), wf=wf)
            c = _gate_cumsum(f, fox_b_f[j], batch=b, seq=t)
            gq = fox_q_g[j].astype(F32) * (FOX_D ** -0.5 * LOG2E)
            qa, ka = _fox_prep(proj, c, gq, fox_k_g[j], width=d)
            att = _fox_attn(qa.reshape(b, t, -1), ka.reshape(b, t, -1), proj.reshape(b, t, -1),
                            v_col=2 * d)
            att = att.reshape(n, -1)
            xf = _mm2_res(att, att, fox_w_out[j].astype(BF16), xf, a2_block=1)
        hid = _norm_mm(xf, norm2_g[l], mlp_w1[l].astype(BF16), sqrelu=True)
        xf = _mm_res(hid, mlp_w2[l].astype(BF16), xf)
    return xf.reshape(b, t, d)
```

```python
import functools
import math

import jax
import jax.numpy as jnp
from jax import lax
from jax.experimental import pallas as pl
from jax.experimental.pallas import tpu as pltpu

F32 = jnp.float32
BF16 = jnp.bfloat16
EPS = 1e-6
LOG2E = math.log2(math.e)
LANES = 128
BF16_SUBLANES = 16
MIB = 2 ** 20

CONV_WIDTH = 3
DIFF_HEADS = 8
DIFF_D = 64
FOX_HEADS = 16
FOX_D = 128

SCORE_COLS = 256
ROW_BLOCK = BF16_SUBLANES


def _cparams(semantics, vmem_mib):
    return pltpu.CompilerParams(dimension_semantics=semantics, vmem_limit_bytes=vmem_mib * MIB)


def _split3(x):
    hi = x.astype(BF16)
    r1 = x - hi.astype(F32)
    mid = r1.astype(BF16)
    lo = (r1 - mid.astype(F32)).astype(BF16)
    return hi, mid, lo


def _normalize_rows(x_ref, g_ref, h_ref):
    x = x_ref[...]
    ms = jnp.mean(x * x, axis=-1, keepdims=True)
    h_ref[...] = ((x * lax.rsqrt(ms + EPS)) * g_ref[...]).astype(BF16)


def _norm_mm_body(x_ref, g_ref, w_ref, o_ref, h_ref, *, sqrelu):
    @pl.when(pl.program_id(1) == 0)
    def _():
        _normalize_rows(x_ref, g_ref, h_ref)

    y = jnp.dot(h_ref[...], w_ref[...], preferred_element_type=F32)
    if sqrelu:
        y = jnp.square(jnp.maximum(y, 0.0))
    o_ref[...] = y.astype(o_ref.dtype)


def _norm_mm_gate_body(x_ref, g_ref, w_ref, wf_ref, o_ref, f_ref, h_ref):
    @pl.when(pl.program_id(1) == 0)
    def _():
        _normalize_rows(x_ref, g_ref, h_ref)
        f_ref[...] = jnp.dot(h_ref[...], wf_ref[...], preferred_element_type=F32)

    y = jnp.dot(h_ref[...], w_ref[...], preferred_element_type=F32)
    o_ref[...] = y.astype(o_ref.dtype)


def _norm_mm(x, g, w, *, sqrelu=False, wf=None, tm=1024, tn=1024):
    n, d = x.shape
    nout = w.shape[1]
    tm, tn = min(tm, n), min(tn, nout)
    grid = (n // tm, nout // tn)
    x_spec = pl.BlockSpec((tm, d), lambda i, j: (i, 0))
    g_spec = pl.BlockSpec((1, d), lambda i, j: (0, 0))
    w_spec = pl.BlockSpec((d, tn), lambda i, j: (0, j))
    o_spec = pl.BlockSpec((tm, tn), lambda i, j: (i, j))
    scratch = [pltpu.VMEM((tm, d), BF16)]
    params = _cparams(("parallel", "arbitrary"), 48)
    g2 = g.reshape(1, d).astype(F32)
    if wf is None:
        return pl.pallas_call(
            functools.partial(_norm_mm_body, sqrelu=sqrelu),
            grid=grid, in_specs=[x_spec, g_spec, w_spec], out_specs=o_spec,
            out_shape=jax.ShapeDtypeStruct((n, nout), BF16),
            scratch_shapes=scratch, compiler_params=params, name="norm_mm")(x, g2, w)
    nf = wf.shape[1]
    return pl.pallas_call(
        _norm_mm_gate_body,
        grid=grid,
        in_specs=[x_spec, g_spec, w_spec, pl.BlockSpec((d, nf), lambda i, j: (0, 0))],
        out_specs=[o_spec, pl.BlockSpec((tm, nf), lambda i, j: (i, 0))],
        out_shape=[jax.ShapeDtypeStruct((n, nout), BF16), jax.ShapeDtypeStruct((n, nf), F32)],
        scratch_shapes=scratch, compiler_params=params, name="norm_mm_gate")(x, g2, w, wf)


def _mm_res_body(a_ref, w_ref, r_ref, o_ref):
    d = jnp.dot(a_ref[...], w_ref[...], preferred_element_type=F32)

    @pl.when(pl.program_id(2) == 0)
    def _():
        o_ref[...] = r_ref[...] + d

    @pl.when(pl.program_id(2) != 0)
    def _():
        o_ref[...] += d


def _mm_res(a, w, r, *, tm=1024, tn=1024, tk=2048):
    n, kdim = a.shape
    nout = w.shape[1]
    tm, tn, tk = min(tm, n), min(tn, nout), min(tk, kdim)
    return pl.pallas_call(
        _mm_res_body,
        grid=(n // tm, nout // tn, kdim // tk),
        in_specs=[pl.BlockSpec((tm, tk), lambda i, j, k: (i, k)),
                  pl.BlockSpec((tk, tn), lambda i, j, k: (k, j)),
                  pl.BlockSpec((tm, tn), lambda i, j, k: (i, j))],
        out_specs=pl.BlockSpec((tm, tn), lambda i, j, k: (i, j)),
        out_shape=jax.ShapeDtypeStruct((n, nout), F32),
        compiler_params=_cparams(("parallel", "parallel", "arbitrary"), 48),
        name="mm_res")(a, w, r)


def _mm2_res_body(a1_ref, a2_ref, w_ref, r_ref, o_ref):
    k1 = a1_ref.shape[1]
    o_ref[...] = (r_ref[...]
                  + jnp.dot(a1_ref[...], w_ref[:k1, :], preferred_element_type=F32)
                  + jnp.dot(a2_ref[...], w_ref[k1:, :], preferred_element_type=F32))


def _mm2_res(a1, a2, w, r, *, a2_block=0, tm=512):
    n = a1.shape[0]
    kdim, nout = w.shape
    k1 = k2 = kdim // 2
    tm = min(tm, n)
    return pl.pallas_call(
        _mm2_res_body,
        grid=(n // tm,),
        in_specs=[pl.BlockSpec((tm, k1), lambda i: (i, 0)),
                  pl.BlockSpec((tm, k2), lambda i: (i, a2_block)),
                  pl.BlockSpec((k1 + k2, nout), lambda i: (0, 0)),
                  pl.BlockSpec((tm, nout), lambda i: (i, 0))],
        out_specs=pl.BlockSpec((tm, nout), lambda i: (i, 0)),
        out_shape=jax.ShapeDtypeStruct((n, nout), F32),
        compiler_params=_cparams(("parallel",), 48),
        name="mm2_res")(a1, a2, w, r)


def _block_diag_ones(group):
    r = lax.broadcasted_iota(jnp.int32, (LANES, LANES), 0) // group
    c = lax.broadcasted_iota(jnp.int32, (LANES, LANES), 1) // group
    return (r == c).astype(BF16)


def _group_rmsnorm(x, gain, bd, group):
    xf = x.astype(F32)
    sq = xf * xf
    hi = sq.astype(BF16)
    lo = (sq - hi.astype(F32)).astype(BF16)
    gs = (jnp.dot(hi, bd, preferred_element_type=F32) + jnp.dot(lo, bd, preferred_element_type=F32))
    return (xf * lax.rsqrt(gs * (1.0 / group) + EPS)) * gain


def _store_transposed(dst_ref, head, row0, x):
    dst_ref[0, head, row0:row0 + LANES, :] = x.T.astype(dst_ref.dtype)


def _store_values_transposed(vt_ref, v_ref, tk):
    tm = v_ref.shape[0]
    for h in range(v_ref.shape[1] // LANES):
        vt = v_ref[:, h * LANES:(h + 1) * LANES].astype(F32).T
        for ch in range(tm // tk):
            vt_ref[0, ch, h * LANES:(h + 1) * LANES, :] = vt[:, ch * tk:(ch + 1) * tk].astype(BF16)


def _hyb_prep_body(gb_ref, gc_ref, u_ref, gcp_ref, up_ref, q_ref, k_ref, v_ref, cw_ref, gq_ref,
                   gk_ref, y_ref, qt_ref, ko_ref, vt_ref, *, tm, seq, tk):
    z = gc_ref[...].astype(F32) * u_ref[...].astype(F32)
    zp = gcp_ref[...].astype(F32) * up_ref[...].astype(F32)
    first_of_seq = lax.rem(pl.program_id(0) * tm, seq) == 0
    zp = jnp.where(first_of_seq, 0.0, zp)
    zext = jnp.concatenate([zp, z], axis=0)
    z1 = pltpu.roll(zext, 1, 0)[BF16_SUBLANES:]
    z2 = pltpu.roll(zext, 2, 0)[BF16_SUBLANES:]
    cw = cw_ref[...]
    y = gb_ref[...].astype(F32) * (cw[0:1] * z2 + cw[1:2] * z1 + cw[2:3] * z)
    y_ref[...] = y.astype(BF16)

    bd = _block_diag_ones(DIFF_D)
    gq, gk = gq_ref[...], gk_ref[...]
    low_half = lax.broadcasted_iota(jnp.int32, (1, LANES), 1) < DIFF_D
    for h in range(q_ref.shape[1] // LANES):
        cols = slice(h * LANES, (h + 1) * LANES)
        _store_transposed(qt_ref, h, 0, _group_rmsnorm(q_ref[:, cols], gq, bd, DIFF_D))
        kn = _group_rmsnorm(k_ref[:, cols], gk, bd, DIFF_D)
        ko_ref[:, 2 * h * LANES:(2 * h + 1) * LANES] = jnp.where(low_half, kn, 0.0).astype(BF16)
        ko_ref[:, (2 * h + 1) * LANES:(2 * h + 2) * LANES] = jnp.where(low_half, 0.0, kn).astype(BF16)
    _store_values_transposed(vt_ref, v_ref, tk)


def _hyb_prep(proj, conv_w, gq, gk, *, batch, seq, tk, tm=512):
    n = proj.shape[0]
    c = conv_w.shape[1]
    tm = min(tm, seq)
    nt = seq // tm
    heads = c // LANES
    halo = BF16_SUBLANES
    prev = lambda col: (lambda i: (jnp.maximum(i * (tm // halo) - 1, 0), col))
    col = lambda b: pl.BlockSpec((tm, c), lambda i: (i, b))
    vec = lambda a: jnp.tile(a.astype(F32), LANES // a.shape[0]).reshape(1, LANES)
    lane_vec = pl.BlockSpec((1, LANES), lambda i: (0, 0))
    return pl.pallas_call(
        functools.partial(_hyb_prep_body, tm=tm, seq=seq, tk=tk),
        grid=(n // tm,),
        in_specs=[col(0), col(1), col(2),
                  pl.BlockSpec((halo, c), prev(1)), pl.BlockSpec((halo, c), prev(2)),
                  col(3), col(4), col(5),
                  pl.BlockSpec((CONV_WIDTH, c), lambda i: (0, 0)), lane_vec, lane_vec],
        out_specs=[pl.BlockSpec((tm, c), lambda i: (i, 0)),
                   pl.BlockSpec((1, heads, LANES, tm), lambda i: (i // nt, 0, 0, i % nt)),
                   pl.BlockSpec((tm, 2 * c), lambda i: (i, 0)),
                   pl.BlockSpec((1, tm // tk, c, tk), lambda i: (i // nt, i % nt, 0, 0))],
        out_shape=[jax.ShapeDtypeStruct((n, c), BF16),
                   jax.ShapeDtypeStruct((batch, heads, LANES, seq), BF16),
                   jax.ShapeDtypeStruct((n, 2 * c), BF16),
                   jax.ShapeDtypeStruct((batch, seq // tk, c, tk), BF16)],
        compiler_params=_cparams(("parallel",), 48),
        name="hyb_prep")(proj, proj, proj, proj, proj, proj, proj, proj, conv_w.astype(F32),
                         vec(gq), vec(gk))


def _fox_prep_body(q_ref, k_ref, v_ref, c_ref, gq_ref, gk_ref, qt_ref, ko_ref, vt_ref, *, tk):
    bd = _block_diag_ones(FOX_D)
    gq, gk = gq_ref[...], gk_ref[...]
    lane = lax.broadcasted_iota(jnp.int32, (1, LANES), 1)
    ones_q = jnp.where((lane >= 3) & (lane < 6), 1.0, 0.0)
    ones_k = jnp.where(lane < 3, 1.0, 0.0)
    c_hi, c_mid, c_lo = (t.astype(F32) for t in _split3(c_ref[...]))
    for h in range(q_ref.shape[1] // LANES):
        cols = slice(h * LANES, (h + 1) * LANES)
        _store_transposed(qt_ref, h, 0, _group_rmsnorm(q_ref[:, cols], gq, bd, FOX_D))
        ko_ref[:, 2 * h * LANES:(2 * h + 1) * LANES] = _group_rmsnorm(k_ref[:, cols], gk, bd, FOX_D).astype(BF16)
        hi, mid, lo = c_hi[:, h:h + 1], c_mid[:, h:h + 1], c_lo[:, h:h + 1]
        split_q = jnp.where(lane == 0, hi, jnp.where(lane == 1, mid, jnp.where(lane == 2, lo, 0.0)))
        split_k = jnp.where(lane == 3, hi, jnp.where(lane == 4, mid, jnp.where(lane == 5, lo, 0.0)))
        _store_transposed(qt_ref, h, LANES, split_q + ones_q)
        ko_ref[:, (2 * h + 1) * LANES:(2 * h + 2) * LANES] = (ones_k - split_k).astype(BF16)
    _store_values_transposed(vt_ref, v_ref, tk)


def _fox_prep(proj, c, gq, gk, *, batch, seq, width, tk, tm=512):
    n = proj.shape[0]
    tm = min(tm, seq)
    nt = seq // tm
    heads = width // LANES
    vec = lambda a: a.astype(F32).reshape(1, LANES)
    lane_vec = pl.BlockSpec((1, LANES), lambda i: (0, 0))
    return pl.pallas_call(
        functools.partial(_fox_prep_body, tk=tk),
        grid=(n // tm,),
        in_specs=[pl.BlockSpec((tm, width), lambda i: (i, 0)),
                  pl.BlockSpec((tm, width), lambda i: (i, 1)),
                  pl.BlockSpec((tm, width), lambda i: (i, 2)),
                  pl.BlockSpec((tm, LANES), lambda i: (i, 0)), lane_vec, lane_vec],
        out_specs=[pl.BlockSpec((1, heads, 2 * LANES, tm), lambda i: (i // nt, 0, 0, i % nt)),
                   pl.BlockSpec((tm, 2 * width), lambda i: (i, 0)),
                   pl.BlockSpec((1, tm // tk, width, tk), lambda i: (i // nt, i % nt, 0, 0))],
        out_shape=[jax.ShapeDtypeStruct((batch, heads, 2 * LANES, seq), BF16),
                   jax.ShapeDtypeStruct((n, 2 * width), BF16),
                   jax.ShapeDtypeStruct((batch, seq // tk, width, tk), BF16)],
        compiler_params=_cparams(("parallel",), 56),
        name="fox_prep")(proj, proj, proj, c, vec(gq), vec(gk))


def _gate_cumsum_body(f_ref, b_ref, c_ref, tri_ref, carry_ref, *, tc):
    @pl.when(pl.program_id(1) == 0)
    def _():
        carry_ref[...] = jnp.zeros_like(carry_ref)
        r = lax.broadcasted_iota(jnp.int32, (tc, tc), 0)
        c = lax.broadcasted_iota(jnp.int32, (tc, tc), 1)
        tri_ref[...] = (r >= c).astype(BF16)

    f = f_ref[...] + b_ref[...]
    logf = -(jnp.maximum(-f, 0.0) + jnp.log1p(jnp.exp(-jnp.abs(f))))
    hi, mid, lo = _split3(logf)
    tri = tri_ref[...]
    c = (jnp.dot(tri, hi, preferred_element_type=F32) + jnp.dot(tri, mid, preferred_element_type=F32)
         + jnp.dot(tri, lo, preferred_element_type=F32)) + carry_ref[0:1, :]
    carry_ref[...] = jnp.broadcast_to(c[tc - 1:tc, :], carry_ref.shape)
    c_ref[...] = c * LOG2E


def _gate_cumsum(f, b_f, *, batch, seq, tc=512):
    nf = f.shape[1]
    tc = min(tc, seq)
    nt = seq // tc
    b2 = jnp.zeros((1, nf), F32).at[0, :b_f.shape[0]].set(b_f.astype(F32))
    return pl.pallas_call(
        functools.partial(_gate_cumsum_body, tc=tc),
        grid=(batch, nt),
        in_specs=[pl.BlockSpec((tc, nf), lambda b, j: (b * nt + j, 0)),
                  pl.BlockSpec((1, nf), lambda b, j: (0, 0))],
        out_specs=pl.BlockSpec((tc, nf), lambda b, j: (b * nt + j, 0)),
        out_shape=jax.ShapeDtypeStruct(f.shape, F32),
        scratch_shapes=[pltpu.VMEM((tc, tc), BF16), pltpu.VMEM((8, nf), F32)],
        compiler_params=_cparams(("arbitrary", "arbitrary"), 32),
        name="gate_cumsum")(f, b2)


def _flash_core(qt_ref, k_ref, vt_ref, s_ref, p_ref, a_ref, m_ref, l_ref, acc_ref, x_ref,
                *, tk, nm, dk):
    qi = pl.program_id(2)
    tq = 2 * tk
    cw = min(SCORE_COLS, tk)
    all_tiles = [slice(a, a + cw) for a in range(0, tq, cw)]
    lower_tiles = [t for t in all_tiles if t.start >= tk]
    key_minus_query = (lax.broadcasted_iota(jnp.int32, (ROW_BLOCK, cw), 0)
                       - lax.broadcasted_iota(jnp.int32, (ROW_BLOCK, cw), 1))

    def visible_rows(tile, key_col):
        return tk if key_col is None else min(tk, tile.start - key_col + cw)

    def scores(c, slot, tiles=all_tiles, key_col=None):
        start = pl.multiple_of(c * tk, tk)
        for m in range(nm):
            if key_col is None:
                span = slice(tiles[0].start, tiles[-1].stop)
                k = k_ref[0, pl.ds(start, tk), m * dk:(m + 1) * dk]
                s_all = jnp.dot(k, qt_ref[0, 0, :, span], preferred_element_type=F32)
                for t in tiles:
                    s = s_all[:, t.start - span.start:t.stop - span.start]
                    s_ref[slot * nm + m, :, t] = s
                    mx = s[0:8]
                    for r in range(8, tk, 8):
                        mx = jnp.maximum(mx, s[r:r + 8])
                    x_ref[slot * nm + m, :, t] = jnp.max(mx, axis=0, keepdims=True)
            else:
                for t in tiles:
                    rows = visible_rows(t, key_col)
                    k = k_ref[0, pl.ds(start, rows), m * dk:(m + 1) * dk]
                    s_ref[slot * nm + m, 0:rows, t] = jnp.dot(k, qt_ref[0, 0, :, t],
                                                              preferred_element_type=F32)

    def softmax(slot, tiles=all_tiles, key_col=None):
        for m in range(nm):
            idx = slot * nm + m
            for t in tiles:
                rows = visible_rows(t, key_col)
                rel = None if key_col is None else t.start - key_col

                def block(r):
                    s = s_ref[idx, r:r + ROW_BLOCK, t]
                    if rel is not None and r + ROW_BLOCK - 1 > rel:
                        s = jnp.where(key_minus_query <= rel - r, s, -jnp.inf)
                    return s

                if rel is None or rel >= tk:
                    smax = x_ref[idx, :, t]
                else:
                    mx = block(0)
                    for r in range(ROW_BLOCK, rows, ROW_BLOCK):
                        mx = jnp.maximum(mx, block(r))
                    smax = jnp.max(mx, axis=0, keepdims=True)
                m_old = m_ref[m, :, t]
                m_new = jnp.maximum(m_old, smax)
                alpha = jnp.exp2(m_old - m_new)
                psum = None
                for r in range(0, rows, ROW_BLOCK):
                    p = jnp.exp2(block(r) - m_new)
                    part = p[0:8] + p[8:16]
                    psum = part if psum is None else psum + part
                    p_ref[idx, r:r + ROW_BLOCK, t] = p.astype(BF16)
                m_ref[m, :, t] = m_new
                l_ref[m, :, t] = alpha * l_ref[m, :, t] + jnp.sum(psum, axis=0, keepdims=True)
                a_ref[idx, :, t] = alpha

    def weighted_values(c, slot, tiles=all_tiles, key_col=None):
        for m in range(nm):
            idx = slot * nm + m
            for t in tiles:
                rows = visible_rows(t, key_col)
                acc_ref[m, :, t] = (a_ref[idx, :, t] * acc_ref[m, :, t]
                                    + jnp.dot(vt_ref[0, c, :, 0:rows], p_ref[idx, 0:rows, t],
                                              preferred_element_type=F32))

    m_ref[...] = jnp.full(m_ref.shape, -jnp.inf, F32)
    l_ref[...] = jnp.zeros(l_ref.shape, F32)
    acc_ref[...] = jnp.zeros(acc_ref.shape, F32)
    for m in range(nm):
        p_ref[nm + m] = jnp.zeros(p_ref.shape[1:], BF16)
        a_ref[nm + m] = jnp.ones(a_ref.shape[1:], F32)

    n_vis = 2 * qi
    scores(0, 0)

    def pair(u, carry):
        c = 2 * u
        scores(c + 1, 1)
        softmax(0)
        weighted_values(jnp.maximum(c - 1, 0), 1)
        scores(c + 2, 0)
        softmax(1)
        weighted_values(c, 0)
        return carry

    lax.fori_loop(0, qi, pair, 0)
    scores(n_vis + 1, 1, lower_tiles, key_col=tk)
    softmax(0, key_col=0)
    weighted_values(jnp.maximum(n_vis - 1, 0), 1)
    softmax(1, lower_tiles, key_col=tk)
    weighted_values(n_vis, 0, key_col=0)
    weighted_values(n_vis + 1, 1, lower_tiles, key_col=tk)


def _flash_scratch(tq, tk, nm, dv):
    return [pltpu.VMEM((2 * nm, tk, tq), F32),
            pltpu.VMEM((2 * nm, tk, tq), BF16),
            pltpu.VMEM((2 * nm, 1, tq), F32),
            pltpu.VMEM((nm, 1, tq), F32),
            pltpu.VMEM((nm, 1, tq), F32),
            pltpu.VMEM((nm, dv, tq), F32),
            pltpu.VMEM((2 * nm, 1, tq), F32)]


def _fox_attn_body(qt_ref, k_ref, vt_ref, o_ref, *scratch, tk):
    _flash_core(qt_ref, k_ref, vt_ref, *scratch, tk=tk, nm=1, dk=k_ref.shape[2])
    l_ref, acc_ref = scratch[4], scratch[5]
    o_ref[0] = (acc_ref[0] / l_ref[0]).T.astype(o_ref.dtype)


def _fox_attn(qt, ka, vt, *, tk):
    b, heads, dq, t = qt.shape
    tq = 2 * tk
    return pl.pallas_call(
        functools.partial(_fox_attn_body, tk=tk),
        grid=(b, heads, t // tq),
        in_specs=[pl.BlockSpec((1, 1, dq, tq), lambda bi, h, i: (bi, h, 0, i)),
                  pl.BlockSpec((1, t, dq), lambda bi, h, i: (bi, 0, h)),
                  pl.BlockSpec((1, t // tk, FOX_D, tk), lambda bi, h, i: (bi, 0, h, 0))],
        out_specs=pl.BlockSpec((1, tq, FOX_D), lambda bi, h, i: (bi, i, h)),
        out_shape=jax.ShapeDtypeStruct((b, t, heads * FOX_D), BF16),
        scratch_shapes=_flash_scratch(tq, tk, 1, FOX_D),
        compiler_params=_cparams(("parallel", "parallel", "arbitrary"), 48),
        name="fox_attn")(qt, ka, vt)


def _diff_attn_body(lq1_ref, lk1_ref, lq2_ref, lk2_ref, sg_ref, qt_ref, k_ref, vt_ref, o_ref,
                    *scratch, tk, lam_init):
    _flash_core(qt_ref, k_ref, vt_ref, *scratch, tk=tk, nm=2, dk=qt_ref.shape[2])
    l_ref, acc_ref = scratch[4], scratch[5]
    lam = (jnp.exp(jnp.sum(lq1_ref[...] * lk1_ref[...], keepdims=True))
           - jnp.exp(jnp.sum(lq2_ref[...] * lk2_ref[...], keepdims=True)) + lam_init)
    o = acc_ref[0] / l_ref[0] - lam * (acc_ref[1] / l_ref[1])
    ms = jnp.mean(o * o, axis=0, keepdims=True)
    o = ((o * lax.rsqrt(ms + EPS)) * sg_ref[...]) * (1.0 - lam_init)
    o_ref[0] = o.T.astype(o_ref.dtype)


def _diff_attn(qt, kn, vt, lq1, lk1, lq2, lk2, subln_g, *, lam_init, tk):
    b, heads, dq, t = qt.shape
    tq = 2 * tk
    dv = 2 * DIFF_D
    vec = lambda a: a.reshape(1, -1).astype(F32)
    small = lambda w: pl.BlockSpec((1, w), lambda bi, h, i: (0, 0))
    return pl.pallas_call(
        functools.partial(_diff_attn_body, tk=tk, lam_init=lam_init),
        grid=(b, heads, t // tq),
        in_specs=[small(DIFF_D), small(DIFF_D), small(DIFF_D), small(DIFF_D),
                  pl.BlockSpec((dv, 1), lambda bi, h, i: (0, 0)),
                  pl.BlockSpec((1, 1, dq, tq), lambda bi, h, i: (bi, h, 0, i)),
                  pl.BlockSpec((1, t, 2 * dq), lambda bi, h, i: (bi, 0, h)),
                  pl.BlockSpec((1, t // tk, dv, tk), lambda bi, h, i: (bi, 0, h, 0))],
        out_specs=pl.BlockSpec((1, tq, dv), lambda bi, h, i: (bi, i, h)),
        out_shape=jax.ShapeDtypeStruct((b, t, heads * dv), BF16),
        scratch_shapes=_flash_scratch(tq, tk, 2, dv),
        compiler_params=_cparams(("parallel", "parallel", "arbitrary"), 48),
        name="diff_attn")(vec(lq1), vec(lk1), vec(lq2), vec(lk2),
                          subln_g.reshape(dv, 1).astype(F32), qt, kn, vt)


def kernel(x, norm1_g, norm2_g, hyb_w_in, hyb_conv_w, hyb_dq_g, hyb_dk_g, hyb_lq1, hyb_lk1,
           hyb_lq2, hyb_lk2, hyb_subln_g, hyb_w_out, fox_w_in, fox_b_f, fox_q_g, fox_k_g,
           fox_w_out, mlp_w1, mlp_w2):
    b, t, d = x.shape
    n = b * t
    depth = norm1_g.shape[0]
    diff_tk = min(256, t // 2)
    fox_tk = min(512, t // 2)
    xf = x.reshape(n, d)
    for l in range(depth):
        j = l // 2
        if l % 2 == 0:
            proj = _norm_mm(xf, norm1_g[l], hyb_w_in[j].astype(BF16))
            gq = hyb_dq_g[j].astype(F32) * (DIFF_D ** -0.5 * LOG2E)
            y_conv, qt, kn, vt = _hyb_prep(proj, hyb_conv_w[j], gq, hyb_dk_g[j], batch=b, seq=t,
                                           tk=diff_tk)
            att = _diff_attn(qt, kn.reshape(b, t, -1), vt, hyb_lq1[j], hyb_lk1[j], hyb_lq2[j],
                             hyb_lk2[j], hyb_subln_g[j], lam_init=0.8 - 0.6 * math.exp(-0.3 * l),
                             tk=diff_tk)
            xf = _mm2_res(y_conv, att.reshape(n, -1), hyb_w_out[j].astype(BF16), xf)
        else:
            w_in = fox_w_in[j]
            wf = jnp.zeros((d, LANES), BF16).at[:, :FOX_HEADS].set(w_in[:, 3 * d:].astype(BF16))
            proj, f = _norm_mm(xf, norm1_g[l], w_in[:, :3 * d].astype(BF16), wf=wf)
            c = _gate_cumsum(f, fox_b_f[j], batch=b, seq=t)
            gq = fox_q_g[j].astype(F32) * (FOX_D ** -0.5 * LOG2E)
            qt, ka, vt = _fox_prep(proj, c, gq, fox_k_g[j], batch=b, seq=t, width=d, tk=fox_tk)
            att = _fox_attn(qt, ka.reshape(b, t, -1), vt, tk=fox_tk).reshape(n, -1)
            xf = _mm2_res(att, att, fox_w_out[j].astype(BF16), xf, a2_block=1)
        hid = _norm_mm(xf, norm2_g[l], mlp_w1[l].astype(BF16), sqrelu=True)
        xf = _mm_res(hid, mlp_w2[l].astype(BF16), xf)
    return xf.reshape(b, t, d)
```

```python
import functools
import math

import jax
import jax.numpy as jnp
from jax import lax
from jax.experimental import pallas as pl
from jax.experimental.pallas import tpu as pltpu

F32 = jnp.float32
BF16 = jnp.bfloat16
EPS = 1e-6
LOG2E = math.log2(math.e)
LANES = 128
BF16_SUBLANES = 16
MIB = 2 ** 20

CONV_WIDTH = 3
DIFF_HEADS = 8
DIFF_D = 64
FOX_HEADS = 16
FOX_D = 128

SCORE_COLS = 256
ROW_BLOCK = BF16_SUBLANES


def _cparams(semantics, vmem_mib):
    return pltpu.CompilerParams(dimension_semantics=semantics, vmem_limit_bytes=vmem_mib * MIB)


def _split3(x):
    hi = x.astype(BF16)
    r1 = x - hi.astype(F32)
    mid = r1.astype(BF16)
    lo = (r1 - mid.astype(F32)).astype(BF16)
    return hi, mid, lo


CAST_BLOCK_ELEMS = 2 ** 21


def _cast_body(w_ref, o_ref):
    o_ref[...] = w_ref[0].astype(o_ref.dtype)


def _to_bf16(w_stack, layer):
    _, rows, cols = w_stack.shape
    tr = min(rows, max(BF16_SUBLANES, CAST_BLOCK_ELEMS // cols // BF16_SUBLANES * BF16_SUBLANES))
    while rows % tr:
        tr -= BF16_SUBLANES
    return pl.pallas_call(
        _cast_body,
        grid=(rows // tr,),
        in_specs=[pl.BlockSpec((1, tr, cols), lambda i: (layer, i, 0))],
        out_specs=pl.BlockSpec((tr, cols), lambda i: (i, 0)),
        out_shape=jax.ShapeDtypeStruct((rows, cols), BF16),
        compiler_params=_cparams(("parallel",), 40),
        name="to_bf16")(w_stack)


def _normalize_rows(x_ref, g_ref, h_ref):
    x = x_ref[...]
    ms = jnp.mean(x * x, axis=-1, keepdims=True)
    h_ref[...] = ((x * lax.rsqrt(ms + EPS)) * g_ref[...]).astype(BF16)


def _norm_mm_body(x_ref, g_ref, w_ref, o_ref, h_ref, *, sqrelu):
    @pl.when(pl.program_id(1) == 0)
    def _():
        _normalize_rows(x_ref, g_ref, h_ref)

    y = jnp.dot(h_ref[...], w_ref[...], preferred_element_type=F32)
    if sqrelu:
        y = jnp.square(jnp.maximum(y, 0.0))
    o_ref[...] = y.astype(o_ref.dtype)


def _norm_mm_gate_body(x_ref, g_ref, w_ref, wf_ref, o_ref, f_ref, h_ref):
    @pl.when(pl.program_id(1) == 0)
    def _():
        _normalize_rows(x_ref, g_ref, h_ref)
        f_ref[...] = jnp.dot(h_ref[...], wf_ref[...], preferred_element_type=F32)

    y = jnp.dot(h_ref[...], w_ref[...], preferred_element_type=F32)
    o_ref[...] = y.astype(o_ref.dtype)


def _norm_mm(x, g, w, *, nout=None, sqrelu=False, wf=None, tm=1024, tn=1024):
    n, d = x.shape
    nout = w.shape[1] if nout is None else nout
    tm, tn = min(tm, n), min(tn, nout)
    grid = (n // tm, nout // tn)
    x_spec = pl.BlockSpec((tm, d), lambda i, j: (i, 0))
    g_spec = pl.BlockSpec((1, d), lambda i, j: (0, 0))
    w_spec = pl.BlockSpec((d, tn), lambda i, j: (0, j))
    o_spec = pl.BlockSpec((tm, tn), lambda i, j: (i, j))
    scratch = [pltpu.VMEM((tm, d), BF16)]
    params = _cparams(("parallel", "arbitrary"), 48)
    g2 = g.reshape(1, d).astype(F32)
    if wf is None:
        return pl.pallas_call(
            functools.partial(_norm_mm_body, sqrelu=sqrelu),
            grid=grid, in_specs=[x_spec, g_spec, w_spec], out_specs=o_spec,
            out_shape=jax.ShapeDtypeStruct((n, nout), BF16),
            scratch_shapes=scratch, compiler_params=params, name="norm_mm")(x, g2, w)
    nf = wf.shape[1]
    return pl.pallas_call(
        _norm_mm_gate_body,
        grid=grid,
        in_specs=[x_spec, g_spec, w_spec, pl.BlockSpec((d, nf), lambda i, j: (0, 0))],
        out_specs=[o_spec, pl.BlockSpec((tm, nf), lambda i, j: (i, 0))],
        out_shape=[jax.ShapeDtypeStruct((n, nout), BF16), jax.ShapeDtypeStruct((n, nf), F32)],
        scratch_shapes=scratch, compiler_params=params, name="norm_mm_gate")(x, g2, w, wf)


def _mm_res_body(a_ref, w_ref, r_ref, o_ref):
    d = jnp.dot(a_ref[...], w_ref[...], preferred_element_type=F32)

    @pl.when(pl.program_id(2) == 0)
    def _():
        o_ref[...] = r_ref[...] + d

    @pl.when(pl.program_id(2) != 0)
    def _():
        o_ref[...] += d


def _mm_res(a, w, r, *, tm=1024, tn=1024, tk=2048):
    n, kdim = a.shape
    nout = w.shape[1]
    tm, tn, tk = min(tm, n), min(tn, nout), min(tk, kdim)
    return pl.pallas_call(
        _mm_res_body,
        grid=(n // tm, nout // tn, kdim // tk),
        in_specs=[pl.BlockSpec((tm, tk), lambda i, j, k: (i, k)),
                  pl.BlockSpec((tk, tn), lambda i, j, k: (k, j)),
                  pl.BlockSpec((tm, tn), lambda i, j, k: (i, j))],
        out_specs=pl.BlockSpec((tm, tn), lambda i, j, k: (i, j)),
        out_shape=jax.ShapeDtypeStruct((n, nout), F32),
        compiler_params=_cparams(("parallel", "parallel", "arbitrary"), 48),
        name="mm_res")(a, w, r)


def _mm2_res_body(a1_ref, a2_ref, w_ref, r_ref, o_ref):
    k1 = a1_ref.shape[1]
    o_ref[...] = (r_ref[...]
                  + jnp.dot(a1_ref[...], w_ref[:k1, :], preferred_element_type=F32)
                  + jnp.dot(a2_ref[...], w_ref[k1:, :], preferred_element_type=F32))


def _mm2_res(a1, a2, w, r, *, a2_block=0, tm=512):
    n = a1.shape[0]
    kdim, nout = w.shape
    k1 = k2 = kdim // 2
    tm = min(tm, n)
    return pl.pallas_call(
        _mm2_res_body,
        grid=(n // tm,),
        in_specs=[pl.BlockSpec((tm, k1), lambda i: (i, 0)),
                  pl.BlockSpec((tm, k2), lambda i: (i, a2_block)),
                  pl.BlockSpec((k1 + k2, nout), lambda i: (0, 0)),
                  pl.BlockSpec((tm, nout), lambda i: (i, 0))],
        out_specs=pl.BlockSpec((tm, nout), lambda i: (i, 0)),
        out_shape=jax.ShapeDtypeStruct((n, nout), F32),
        compiler_params=_cparams(("parallel",), 48),
        name="mm2_res")(a1, a2, w, r)


def _block_diag_ones(group):
    r = (lax.broadcasted_iota(jnp.int32, (2 * LANES, LANES), 0) % LANES) // group
    c = lax.broadcasted_iota(jnp.int32, (2 * LANES, LANES), 1) // group
    return (r == c).astype(BF16)


def _group_rmsnorm(x, gain, bd, group):
    xf = x.astype(F32)
    sq = xf * xf
    hi = sq.astype(BF16)
    lo = (sq - hi.astype(F32)).astype(BF16)
    gs = jnp.dot(jnp.concatenate([hi, lo], axis=1), bd, preferred_element_type=F32)
    return (xf * lax.rsqrt(gs * (1.0 / group) + EPS)) * gain


def _store_transposed(dst_ref, head, row0, x):
    dst_ref[0, head, row0:row0 + LANES, :] = x.astype(dst_ref.dtype).T


def _store_values_transposed(vt_ref, v_ref, tk):
    tm = v_ref.shape[0]
    for h in range(v_ref.shape[1] // LANES):
        vt = v_ref[:, h * LANES:(h + 1) * LANES].T
        for ch in range(tm // tk):
            vt_ref[0, ch, h * LANES:(h + 1) * LANES, :] = vt[:, ch * tk:(ch + 1) * tk]


def _hyb_prep_body(gb_ref, gc_ref, u_ref, gcp_ref, up_ref, q_ref, k_ref, v_ref, cw_ref, gq_ref,
                   gk_ref, y_ref, qt_ref, ko_ref, vt_ref, *, tm, seq, tk):
    z = gc_ref[...].astype(F32) * u_ref[...].astype(F32)
    zp = gcp_ref[...].astype(F32) * up_ref[...].astype(F32)
    first_of_seq = lax.rem(pl.program_id(0) * tm, seq) == 0
    zp = jnp.where(first_of_seq, 0.0, zp)
    zext = jnp.concatenate([zp, z], axis=0)
    z1 = pltpu.roll(zext, 1, 0)[BF16_SUBLANES:]
    z2 = pltpu.roll(zext, 2, 0)[BF16_SUBLANES:]
    cw = cw_ref[...]
    y = gb_ref[...].astype(F32) * (cw[0:1] * z2 + cw[1:2] * z1 + cw[2:3] * z)
    y_ref[...] = y.astype(BF16)

    bd = _block_diag_ones(DIFF_D)
    gq, gk = gq_ref[...], gk_ref[...]
    low_half = lax.broadcasted_iota(jnp.int32, (1, LANES), 1) < DIFF_D
    for h in range(q_ref.shape[1] // LANES):
        cols = slice(h * LANES, (h + 1) * LANES)
        _store_transposed(qt_ref, h, 0, _group_rmsnorm(q_ref[:, cols], gq, bd, DIFF_D))
        kn = _group_rmsnorm(k_ref[:, cols], gk, bd, DIFF_D)
        ko_ref[:, 2 * h * LANES:(2 * h + 1) * LANES] = jnp.where(low_half, kn, 0.0).astype(BF16)
        ko_ref[:, (2 * h + 1) * LANES:(2 * h + 2) * LANES] = jnp.where(low_half, 0.0, kn).astype(BF16)
    _store_values_transposed(vt_ref, v_ref, tk)


def _hyb_prep(proj, conv_w, gq, gk, *, batch, seq, tk, tm=512):
    n = proj.shape[0]
    c = conv_w.shape[1]
    tm = min(tm, seq)
    nt = seq // tm
    heads = c // LANES
    halo = BF16_SUBLANES
    prev = lambda col: (lambda i: (jnp.maximum(i * (tm // halo) - 1, 0), col))
    col = lambda b: pl.BlockSpec((tm, c), lambda i: (i, b))
    vec = lambda a: jnp.tile(a.astype(F32), LANES // a.shape[0]).reshape(1, LANES)
    lane_vec = pl.BlockSpec((1, LANES), lambda i: (0, 0))
    return pl.pallas_call(
        functools.partial(_hyb_prep_body, tm=tm, seq=seq, tk=tk),
        grid=(n // tm,),
        in_specs=[col(0), col(1), col(2),
                  pl.BlockSpec((halo, c), prev(1)), pl.BlockSpec((halo, c), prev(2)),
                  col(3), col(4), col(5),
                  pl.BlockSpec((CONV_WIDTH, c), lambda i: (0, 0)), lane_vec, lane_vec],
        out_specs=[pl.BlockSpec((tm, c), lambda i: (i, 0)),
                   pl.BlockSpec((1, heads, LANES, tm), lambda i: (i // nt, 0, 0, i % nt)),
                   pl.BlockSpec((tm, 2 * c), lambda i: (i, 0)),
                   pl.BlockSpec((1, tm // tk, c, tk), lambda i: (i // nt, i % nt, 0, 0))],
        out_shape=[jax.ShapeDtypeStruct((n, c), BF16),
                   jax.ShapeDtypeStruct((batch, heads, LANES, seq), BF16),
                   jax.ShapeDtypeStruct((n, 2 * c), BF16),
                   jax.ShapeDtypeStruct((batch, seq // tk, c, tk), BF16)],
        compiler_params=_cparams(("parallel",), 48),
        name="hyb_prep")(proj, proj, proj, proj, proj, proj, proj, proj, conv_w.astype(F32),
                         vec(gq), vec(gk))


def _fox_prep_body(q_ref, k_ref, v_ref, c_ref, gq_ref, gk_ref, qt_ref, ko_ref, vt_ref, *, tk):
    bd = _block_diag_ones(FOX_D)
    gq, gk = gq_ref[...], gk_ref[...]
    lane = lax.broadcasted_iota(jnp.int32, (1, LANES), 1)
    ones_q = jnp.where((lane >= 3) & (lane < 6), 1.0, 0.0)
    ones_k = jnp.where(lane < 3, 1.0, 0.0)
    c_hi, c_mid, c_lo = (t.astype(F32) for t in _split3(c_ref[...]))
    for h in range(q_ref.shape[1] // LANES):
        cols = slice(h * LANES, (h + 1) * LANES)
        _store_transposed(qt_ref, h, 0, _group_rmsnorm(q_ref[:, cols], gq, bd, FOX_D))
        ko_ref[:, 2 * h * LANES:(2 * h + 1) * LANES] = _group_rmsnorm(k_ref[:, cols], gk, bd, FOX_D).astype(BF16)
        hi, mid, lo = c_hi[:, h:h + 1], c_mid[:, h:h + 1], c_lo[:, h:h + 1]
        split_q = jnp.where(lane == 0, hi, jnp.where(lane == 1, mid, jnp.where(lane == 2, lo, 0.0)))
        split_k = jnp.where(lane == 3, hi, jnp.where(lane == 4, mid, jnp.where(lane == 5, lo, 0.0)))
        _store_transposed(qt_ref, h, LANES, split_q + ones_q)
        ko_ref[:, (2 * h + 1) * LANES:(2 * h + 2) * LANES] = (ones_k - split_k).astype(BF16)
    _store_values_transposed(vt_ref, v_ref, tk)


def _fox_prep(proj, c, gq, gk, *, batch, seq, width, tk, tm=512):
    n = proj.shape[0]
    tm = min(tm, seq)
    nt = seq // tm
    heads = width // LANES
    vec = lambda a: a.astype(F32).reshape(1, LANES)
    lane_vec = pl.BlockSpec((1, LANES), lambda i: (0, 0))
    return pl.pallas_call(
        functools.partial(_fox_prep_body, tk=tk),
        grid=(n // tm,),
        in_specs=[pl.BlockSpec((tm, width), lambda i: (i, 0)),
                  pl.BlockSpec((tm, width), lambda i: (i, 1)),
                  pl.BlockSpec((tm, width), lambda i: (i, 2)),
                  pl.BlockSpec((tm, LANES), lambda i: (i, 0)), lane_vec, lane_vec],
        out_specs=[pl.BlockSpec((1, heads, 2 * LANES, tm), lambda i: (i // nt, 0, 0, i % nt)),
                   pl.BlockSpec((tm, 2 * width), lambda i: (i, 0)),
                   pl.BlockSpec((1, tm // tk, width, tk), lambda i: (i // nt, i % nt, 0, 0))],
        out_shape=[jax.ShapeDtypeStruct((batch, heads, 2 * LANES, seq), BF16),
                   jax.ShapeDtypeStruct((n, 2 * width), BF16),
                   jax.ShapeDtypeStruct((batch, seq // tk, width, tk), BF16)],
        compiler_params=_cparams(("parallel",), 56),
        name="fox_prep")(proj, proj, proj, c, vec(gq), vec(gk))


def _gate_cumsum_body(f_ref, b_ref, c_ref, tri_ref, carry_ref, *, tc):
    @pl.when(pl.program_id(1) == 0)
    def _():
        carry_ref[...] = jnp.zeros_like(carry_ref)
        r = lax.broadcasted_iota(jnp.int32, (tc, tc), 0)
        c = lax.broadcasted_iota(jnp.int32, (tc, tc), 1)
        tri_ref[...] = (r >= c).astype(BF16)

    f = f_ref[...] + b_ref[...]
    logf = -(jnp.maximum(-f, 0.0) + jnp.log1p(jnp.exp(-jnp.abs(f))))
    hi, mid, lo = _split3(logf)
    tri = tri_ref[...]
    c = (jnp.dot(tri, hi, preferred_element_type=F32) + jnp.dot(tri, mid, preferred_element_type=F32)
         + jnp.dot(tri, lo, preferred_element_type=F32)) + carry_ref[0:1, :]
    carry_ref[...] = jnp.broadcast_to(c[tc - 1:tc, :], carry_ref.shape)
    c_ref[...] = c * LOG2E


def _gate_cumsum(f, b_f, *, batch, seq, tc=512):
    nf = f.shape[1]
    tc = min(tc, seq)
    nt = seq // tc
    b2 = jnp.zeros((1, nf), F32).at[0, :b_f.shape[0]].set(b_f.astype(F32))
    return pl.pallas_call(
        functools.partial(_gate_cumsum_body, tc=tc),
        grid=(batch, nt),
        in_specs=[pl.BlockSpec((tc, nf), lambda b, j: (b * nt + j, 0)),
                  pl.BlockSpec((1, nf), lambda b, j: (0, 0))],
        out_specs=pl.BlockSpec((tc, nf), lambda b, j: (b * nt + j, 0)),
        out_shape=jax.ShapeDtypeStruct(f.shape, F32),
        scratch_shapes=[pltpu.VMEM((tc, tc), BF16), pltpu.VMEM((8, nf), F32)],
        compiler_params=_cparams(("arbitrary", "arbitrary"), 32),
        name="gate_cumsum")(f, b2)


def _flash_core(qt_ref, k_ref, vt_ref, s_ref, p_ref, a_ref, m_ref, l_ref, acc_ref, x_ref,
                *, tk, nm, dk):
    qi = pl.program_id(2)
    tq = 2 * tk
    cw = min(SCORE_COLS, tk)
    all_tiles = [slice(a, a + cw) for a in range(0, tq, cw)]
    lower_tiles = [t for t in all_tiles if t.start >= tk]
    key_minus_query = (lax.broadcasted_iota(jnp.int32, (ROW_BLOCK, cw), 0)
                       - lax.broadcasted_iota(jnp.int32, (ROW_BLOCK, cw), 1))

    def visible_rows(tile, key_col):
        return tk if key_col is None else min(tk, tile.start - key_col + cw)

    def scores(c, slot, tiles=all_tiles, key_col=None):
        start = pl.multiple_of(c * tk, tk)

        def keys(rows):
            return jnp.concatenate([k_ref[0, pl.ds(start, rows), m * dk:(m + 1) * dk]
                                    for m in range(nm)], axis=0)

        if key_col is None:
            span = slice(tiles[0].start, tiles[-1].stop)
            s_all = jnp.dot(keys(tk), qt_ref[0, 0, :, span], preferred_element_type=F32)
            for m in range(nm):
                for t in tiles:
                    s = s_all[m * tk:(m + 1) * tk, t.start - span.start:t.stop - span.start]
                    s_ref[slot * nm + m, :, t] = s
                    mx = s[0:8]
                    for r in range(8, tk, 8):
                        mx = jnp.maximum(mx, s[r:r + 8])
                    x_ref[slot * nm + m, :, t] = jnp.max(mx, axis=0, keepdims=True)
        else:
            for t in tiles:
                rows = visible_rows(t, key_col)
                s = jnp.dot(keys(rows), qt_ref[0, 0, :, t], preferred_element_type=F32)
                for m in range(nm):
                    s_ref[slot * nm + m, 0:rows, t] = s[m * rows:(m + 1) * rows]

    def softmax(slot, tiles=all_tiles, key_col=None):
        for m in range(nm):
            idx = slot * nm + m
            for t in tiles:
                rows = visible_rows(t, key_col)
                rel = None if key_col is None else t.start - key_col

                def block(r):
                    s = s_ref[idx, r:r + ROW_BLOCK, t]
                    if rel is not None and r + ROW_BLOCK - 1 > rel:
                        s = jnp.where(key_minus_query <= rel - r, s, -jnp.inf)
                    return s

                if rel is None or rel >= tk:
                    smax = x_ref[idx, :, t]
                else:
                    mx = block(0)
                    for r in range(ROW_BLOCK, rows, ROW_BLOCK):
                        mx = jnp.maximum(mx, block(r))
                    smax = jnp.max(mx, axis=0, keepdims=True)
                m_old = m_ref[m, :, t]
                m_new = jnp.maximum(m_old, smax)
                alpha = jnp.exp2(m_old - m_new)
                psum = None
                for r in range(0, rows, ROW_BLOCK):
                    p = jnp.exp2(block(r) - m_new)
                    part = p[0:8] + p[8:16]
                    psum = part if psum is None else psum + part
                    p_ref[idx, r:r + ROW_BLOCK, t] = p.astype(BF16)
                m_ref[m, :, t] = m_new
                l_ref[m, :, t] = alpha * l_ref[m, :, t] + jnp.sum(psum, axis=0, keepdims=True)
                a_ref[idx, :, t] = alpha

    def weighted_values(c, slot, tiles=all_tiles, key_col=None):
        for m in range(nm):
            idx = slot * nm + m
            for t in tiles:
                rows = visible_rows(t, key_col)
                acc_ref[m, :, t] = (a_ref[idx, :, t] * acc_ref[m, :, t]
                                    + jnp.dot(vt_ref[0, c, :, 0:rows], p_ref[idx, 0:rows, t],
                                              preferred_element_type=F32))

    m_ref[...] = jnp.full(m_ref.shape, -jnp.inf, F32)
    l_ref[...] = jnp.zeros(l_ref.shape, F32)
    acc_ref[...] = jnp.zeros(acc_ref.shape, F32)
    for m in range(nm):
        p_ref[nm + m] = jnp.zeros(p_ref.shape[1:], BF16)
        a_ref[nm + m] = jnp.ones(a_ref.shape[1:], F32)

    n_vis = 2 * qi
    scores(0, 0)

    def pair(u, carry):
        c = 2 * u
        scores(c + 1, 1)
        softmax(0)
        weighted_values(jnp.maximum(c - 1, 0), 1)
        scores(c + 2, 0)
        softmax(1)
        weighted_values(c, 0)
        return carry

    lax.fori_loop(0, qi, pair, 0)
    scores(n_vis + 1, 1, lower_tiles, key_col=tk)
    softmax(0, key_col=0)
    weighted_values(jnp.maximum(n_vis - 1, 0), 1)
    softmax(1, lower_tiles, key_col=tk)
    weighted_values(n_vis, 0, key_col=0)
    weighted_values(n_vis + 1, 1, lower_tiles, key_col=tk)


def _flash_scratch(tq, tk, nm, dv):
    return [pltpu.VMEM((2 * nm, tk, tq), F32),
            pltpu.VMEM((2 * nm, tk, tq), BF16),
            pltpu.VMEM((2 * nm, 1, tq), F32),
            pltpu.VMEM((nm, 1, tq), F32),
            pltpu.VMEM((nm, 1, tq), F32),
            pltpu.VMEM((nm, dv, tq), F32),
            pltpu.VMEM((2 * nm, 1, tq), F32)]


def _fox_attn_body(qt_ref, k_ref, vt_ref, o_ref, *scratch, tk):
    _flash_core(qt_ref, k_ref, vt_ref, *scratch, tk=tk, nm=1, dk=k_ref.shape[2])
    l_ref, acc_ref = scratch[4], scratch[5]
    o_ref[0] = (acc_ref[0] / l_ref[0]).T.astype(o_ref.dtype)


def _fox_attn(qt, ka, vt, *, tk):
    b, heads, dq, t = qt.shape
    tq = 2 * tk
    return pl.pallas_call(
        functools.partial(_fox_attn_body, tk=tk),
        grid=(b, heads, t // tq),
        in_specs=[pl.BlockSpec((1, 1, dq, tq), lambda bi, h, i: (bi, h, 0, i)),
                  pl.BlockSpec((1, t, dq), lambda bi, h, i: (bi, 0, h)),
                  pl.BlockSpec((1, t // tk, FOX_D, tk), lambda bi, h, i: (bi, 0, h, 0))],
        out_specs=pl.BlockSpec((1, tq, FOX_D), lambda bi, h, i: (bi, i, h)),
        out_shape=jax.ShapeDtypeStruct((b, t, heads * FOX_D), BF16),
        scratch_shapes=_flash_scratch(tq, tk, 1, FOX_D),
        compiler_params=_cparams(("parallel", "parallel", "arbitrary"), 48),
        name="fox_attn")(qt, ka, vt)


def _diff_attn_body(lq1_ref, lk1_ref, lq2_ref, lk2_ref, sg_ref, qt_ref, k_ref, vt_ref, o_ref,
                    *scratch, tk, lam_init):
    _flash_core(qt_ref, k_ref, vt_ref, *scratch, tk=tk, nm=2, dk=qt_ref.shape[2])
    l_ref, acc_ref = scratch[4], scratch[5]
    lam = (jnp.exp(jnp.sum(lq1_ref[...] * lk1_ref[...], keepdims=True))
           - jnp.exp(jnp.sum(lq2_ref[...] * lk2_ref[...], keepdims=True)) + lam_init)
    o = acc_ref[0] / l_ref[0] - lam * (acc_ref[1] / l_ref[1])
    ms = jnp.mean(o * o, axis=0, keepdims=True)
    o = ((o * lax.rsqrt(ms + EPS)) * sg_ref[...]) * (1.0 - lam_init)
    o_ref[0] = o.T.astype(o_ref.dtype)


def _diff_attn(qt, kn, vt, lq1, lk1, lq2, lk2, subln_g, *, lam_init, tk):
    b, heads, dq, t = qt.shape
    tq = 2 * tk
    dv = 2 * DIFF_D
    vec = lambda a: a.reshape(1, -1).astype(F32)
    small = lambda w: pl.BlockSpec((1, w), lambda bi, h, i: (0, 0))
    return pl.pallas_call(
        functools.partial(_diff_attn_body, tk=tk, lam_init=lam_init),
        grid=(b, heads, t // tq),
        in_specs=[small(DIFF_D), small(DIFF_D), small(DIFF_D), small(DIFF_D),
                  pl.BlockSpec((dv, 1), lambda bi, h, i: (0, 0)),
                  pl.BlockSpec((1, 1, dq, tq), lambda bi, h, i: (bi, h, 0, i)),
                  pl.BlockSpec((1, t, 2 * dq), lambda bi, h, i: (bi, 0, h)),
                  pl.BlockSpec((1, t // tk, dv, tk), lambda bi, h, i: (bi, 0, h, 0))],
        out_specs=pl.BlockSpec((1, tq, dv), lambda bi, h, i: (bi, i, h)),
        out_shape=jax.ShapeDtypeStruct((b, t, heads * dv), BF16),
        scratch_shapes=_flash_scratch(tq, tk, 2, dv),
        compiler_params=_cparams(("parallel", "parallel", "arbitrary"), 48),
        name="diff_attn")(vec(lq1), vec(lk1), vec(lq2), vec(lk2),
                          subln_g.reshape(dv, 1).astype(F32), qt, kn, vt)


def kernel(x, norm1_g, norm2_g, hyb_w_in, hyb_conv_w, hyb_dq_g, hyb_dk_g, hyb_lq1, hyb_lk1,
           hyb_lq2, hyb_lk2, hyb_subln_g, hyb_w_out, fox_w_in, fox_b_f, fox_q_g, fox_k_g,
           fox_w_out, mlp_w1, mlp_w2):
    b, t, d = x.shape
    n = b * t
    depth = norm1_g.shape[0]
    diff_tk = min(256, t // 2)
    fox_tk = min(512, t // 2)
    xf = x.reshape(n, d)
    for l in range(depth):
        j = l // 2
        if l % 2 == 0:
            proj = _norm_mm(xf, norm1_g[l], _to_bf16(hyb_w_in, j))
            gq = hyb_dq_g[j].astype(F32) * (DIFF_D ** -0.5 * LOG2E)
            y_conv, qt, kn, vt = _hyb_prep(proj, hyb_conv_w[j], gq, hyb_dk_g[j], batch=b, seq=t,
                                           tk=diff_tk)
            att = _diff_attn(qt, kn.reshape(b, t, -1), vt, hyb_lq1[j], hyb_lk1[j], hyb_lq2[j],
                             hyb_lk2[j], hyb_subln_g[j], lam_init=0.8 - 0.6 * math.exp(-0.3 * l),
                             tk=diff_tk)
            xf = _mm2_res(y_conv, att.reshape(n, -1), _to_bf16(hyb_w_out, j), xf)
        else:
            w_in = _to_bf16(fox_w_in, j)
            wf = jnp.zeros((d, LANES), BF16).at[:, :FOX_HEADS].set(w_in[:, 3 * d:])
            proj, f = _norm_mm(xf, norm1_g[l], w_in, nout=3 * d, wf=wf)
            c = _gate_cumsum(f, fox_b_f[j], batch=b, seq=t)
            gq = fox_q_g[j].astype(F32) * (FOX_D ** -0.5 * LOG2E)
            qt, ka, vt = _fox_prep(proj, c, gq, fox_k_g[j], batch=b, seq=t, width=d, tk=fox_tk)
            att = _fox_attn(qt, ka.reshape(b, t, -1), vt, tk=fox_tk).reshape(n, -1)
            xf = _mm2_res(att, att, _to_bf16(fox_w_out, j), xf, a2_block=1)
        hid = _norm_mm(xf, norm2_g[l], _to_bf16(mlp_w1, l), sqrelu=True)
        xf = _mm_res(hid, _to_bf16(mlp_w2, l), xf)
    return xf.reshape(b, t, d)
```

```python
import functools
import math

import jax
import jax.numpy as jnp
from jax import lax
from jax.experimental import pallas as pl
from jax.experimental.pallas import tpu as pltpu

F32 = jnp.float32
BF16 = jnp.bfloat16
EPS = 1e-6
LOG2E = math.log2(math.e)
LANES = 128
BF16_SUBLANES = 16
MIB = 2 ** 20

CONV_WIDTH = 3
DIFF_HEADS = 8
DIFF_D = 64
FOX_HEADS = 16
FOX_D = 128

ATTN_KEY_CHUNK = 512
SCORE_COLS = 256
ROW_BLOCK = BF16_SUBLANES


def _cparams(semantics, vmem_mib):
    return pltpu.CompilerParams(dimension_semantics=semantics, vmem_limit_bytes=vmem_mib * MIB)


def _split3(x):
    hi = x.astype(BF16)
    r1 = x - hi.astype(F32)
    mid = r1.astype(BF16)
    lo = (r1 - mid.astype(F32)).astype(BF16)
    return hi, mid, lo


CAST_BLOCK_ELEMS = 2 ** 21


def _cast_body(w_ref, o_ref):
    o_ref[...] = w_ref[0].astype(o_ref.dtype)


def _to_bf16(w_stack, layer):
    _, rows, cols = w_stack.shape
    tr = min(rows, max(BF16_SUBLANES, CAST_BLOCK_ELEMS // cols // BF16_SUBLANES * BF16_SUBLANES))
    while rows % tr:
        tr -= BF16_SUBLANES
    return pl.pallas_call(
        _cast_body,
        grid=(rows // tr,),
        in_specs=[pl.BlockSpec((1, tr, cols), lambda i: (layer, i, 0))],
        out_specs=pl.BlockSpec((tr, cols), lambda i: (i, 0)),
        out_shape=jax.ShapeDtypeStruct((rows, cols), BF16),
        compiler_params=_cparams(("parallel",), 40),
        name="to_bf16")(w_stack)


def _normalize_rows(x_ref, g_ref, h_ref):
    x = x_ref[...]
    ms = jnp.mean(x * x, axis=-1, keepdims=True)
    h_ref[...] = ((x * lax.rsqrt(ms + EPS)) * g_ref[...]).astype(BF16)


def _norm_mm_body(x_ref, g_ref, w_ref, o_ref, h_ref, *, sqrelu):
    @pl.when(pl.program_id(1) == 0)
    def _():
        _normalize_rows(x_ref, g_ref, h_ref)

    y = jnp.dot(h_ref[...], w_ref[...], preferred_element_type=F32)
    if sqrelu:
        y = jnp.square(jnp.maximum(y, 0.0))
    o_ref[...] = y.astype(o_ref.dtype)


def _norm_mm_gate_body(x_ref, g_ref, w_ref, wf_ref, o_ref, f_ref, h_ref):
    @pl.when(pl.program_id(1) == 0)
    def _():
        _normalize_rows(x_ref, g_ref, h_ref)
        f_ref[...] = jnp.dot(h_ref[...], wf_ref[...], preferred_element_type=F32)

    y = jnp.dot(h_ref[...], w_ref[...], preferred_element_type=F32)
    o_ref[...] = y.astype(o_ref.dtype)


def _norm_mm(x, g, w, *, nout=None, sqrelu=False, wf=None, tm=1024, tn=1024):
    n, d = x.shape
    nout = w.shape[1] if nout is None else nout
    tm, tn = min(tm, n), min(tn, nout)
    grid = (n // tm, nout // tn)
    x_spec = pl.BlockSpec((tm, d), lambda i, j: (i, 0))
    g_spec = pl.BlockSpec((1, d), lambda i, j: (0, 0))
    w_spec = pl.BlockSpec((d, tn), lambda i, j: (0, j))
    o_spec = pl.BlockSpec((tm, tn), lambda i, j: (i, j))
    scratch = [pltpu.VMEM((tm, d), BF16)]
    params = _cparams(("parallel", "arbitrary"), 48)
    g2 = g.reshape(1, d).astype(F32)
    if wf is None:
        return pl.pallas_call(
            functools.partial(_norm_mm_body, sqrelu=sqrelu),
            grid=grid, in_specs=[x_spec, g_spec, w_spec], out_specs=o_spec,
            out_shape=jax.ShapeDtypeStruct((n, nout), BF16),
            scratch_shapes=scratch, compiler_params=params, name="norm_mm")(x, g2, w)
    nf = wf.shape[1]
    return pl.pallas_call(
        _norm_mm_gate_body,
        grid=grid,
        in_specs=[x_spec, g_spec, w_spec, pl.BlockSpec((d, nf), lambda i, j: (0, 0))],
        out_specs=[o_spec, pl.BlockSpec((tm, nf), lambda i, j: (i, 0))],
        out_shape=[jax.ShapeDtypeStruct((n, nout), BF16), jax.ShapeDtypeStruct((n, nf), F32)],
        scratch_shapes=scratch, compiler_params=params, name="norm_mm_gate")(x, g2, w, wf)


def _mm_res_body(a_ref, w_ref, r_ref, o_ref):
    d = jnp.dot(a_ref[...], w_ref[...], preferred_element_type=F32)

    @pl.when(pl.program_id(2) == 0)
    def _():
        o_ref[...] = r_ref[...] + d

    @pl.when(pl.program_id(2) != 0)
    def _():
        o_ref[...] += d


def _mm_res(a, w, r, *, tm=1024, tn=1024, tk=2048):
    n, kdim = a.shape
    nout = w.shape[1]
    tm, tn, tk = min(tm, n), min(tn, nout), min(tk, kdim)
    return pl.pallas_call(
        _mm_res_body,
        grid=(n // tm, nout // tn, kdim // tk),
        in_specs=[pl.BlockSpec((tm, tk), lambda i, j, k: (i, k)),
                  pl.BlockSpec((tk, tn), lambda i, j, k: (k, j)),
                  pl.BlockSpec((tm, tn), lambda i, j, k: (i, j))],
        out_specs=pl.BlockSpec((tm, tn), lambda i, j, k: (i, j)),
        out_shape=jax.ShapeDtypeStruct((n, nout), F32),
        compiler_params=_cparams(("parallel", "parallel", "arbitrary"), 48),
        name="mm_res")(a, w, r)


def _mm2_res_body(a1_ref, a2_ref, w_ref, r_ref, o_ref):
    k1 = a1_ref.shape[1]
    o_ref[...] = (r_ref[...]
                  + jnp.dot(a1_ref[...], w_ref[:k1, :], preferred_element_type=F32)
                  + jnp.dot(a2_ref[...], w_ref[k1:, :], preferred_element_type=F32))


def _mm2_res(a1, a2, w, r, *, a2_block=0, tm=512):
    n = a1.shape[0]
    kdim, nout = w.shape
    k1 = k2 = kdim // 2
    tm = min(tm, n)
    return pl.pallas_call(
        _mm2_res_body,
        grid=(n // tm,),
        in_specs=[pl.BlockSpec((tm, k1), lambda i: (i, 0)),
                  pl.BlockSpec((tm, k2), lambda i: (i, a2_block)),
                  pl.BlockSpec((k1 + k2, nout), lambda i: (0, 0)),
                  pl.BlockSpec((tm, nout), lambda i: (i, 0))],
        out_specs=pl.BlockSpec((tm, nout), lambda i: (i, 0)),
        out_shape=jax.ShapeDtypeStruct((n, nout), F32),
        compiler_params=_cparams(("parallel",), 48),
        name="mm2_res")(a1, a2, w, r)


def _block_diag_ones(group):
    r = (lax.broadcasted_iota(jnp.int32, (2 * LANES, LANES), 0) % LANES) // group
    c = lax.broadcasted_iota(jnp.int32, (2 * LANES, LANES), 1) // group
    return (r == c).astype(BF16)


def _group_rmsnorm(x, gain, bd, group):
    xf = x.astype(F32)
    sq = xf * xf
    hi = sq.astype(BF16)
    lo = (sq - hi.astype(F32)).astype(BF16)
    gs = jnp.dot(jnp.concatenate([hi, lo], axis=1), bd, preferred_element_type=F32)
    return (xf * lax.rsqrt(gs * (1.0 / group) + EPS)) * gain


def _store_transposed(dst_ref, head, row0, x):
    dst_ref[0, head, row0:row0 + LANES, :] = x.astype(dst_ref.dtype).T


def _store_values_transposed(vt_ref, v_ref, tk):
    tm = v_ref.shape[0]
    for h in range(v_ref.shape[1] // LANES):
        vt = v_ref[:, h * LANES:(h + 1) * LANES].T
        for ch in range(tm // tk):
            vt_ref[0, ch, h * LANES:(h + 1) * LANES, :] = vt[:, ch * tk:(ch + 1) * tk]


def _hyb_prep_body(gb_ref, gc_ref, u_ref, gcp_ref, up_ref, q_ref, k_ref, v_ref, cw_ref, gq_ref,
                   gk_ref, y_ref, qt_ref, ko_ref, vt_ref, *, tm, seq, tk):
    z = gc_ref[...].astype(F32) * u_ref[...].astype(F32)
    zp = gcp_ref[...].astype(F32) * up_ref[...].astype(F32)
    first_of_seq = lax.rem(pl.program_id(0) * tm, seq) == 0
    zp = jnp.where(first_of_seq, 0.0, zp)
    zext = jnp.concatenate([zp, z], axis=0)
    z1 = pltpu.roll(zext, 1, 0)[BF16_SUBLANES:]
    z2 = pltpu.roll(zext, 2, 0)[BF16_SUBLANES:]
    cw = cw_ref[...]
    y = gb_ref[...].astype(F32) * (cw[0:1] * z2 + cw[1:2] * z1 + cw[2:3] * z)
    y_ref[...] = y.astype(BF16)

    bd = _block_diag_ones(DIFF_D)
    gq, gk = gq_ref[...], gk_ref[...]
    low_half = lax.broadcasted_iota(jnp.int32, (1, LANES), 1) < DIFF_D
    for h in range(q_ref.shape[1] // LANES):
        cols = slice(h * LANES, (h + 1) * LANES)
        _store_transposed(qt_ref, h, 0, _group_rmsnorm(q_ref[:, cols], gq, bd, DIFF_D))
        kn = _group_rmsnorm(k_ref[:, cols], gk, bd, DIFF_D)
        ko_ref[:, 2 * h * LANES:(2 * h + 1) * LANES] = jnp.where(low_half, kn, 0.0).astype(BF16)
        ko_ref[:, (2 * h + 1) * LANES:(2 * h + 2) * LANES] = jnp.where(low_half, 0.0, kn).astype(BF16)
    _store_values_transposed(vt_ref, v_ref, tk)


def _hyb_prep(proj, conv_w, gq, gk, *, batch, seq, tk, tm=512):
    n = proj.shape[0]
    c = conv_w.shape[1]
    tm = min(tm, seq)
    nt = seq // tm
    heads = c // LANES
    halo = BF16_SUBLANES
    prev = lambda col: (lambda i: (jnp.maximum(i * (tm // halo) - 1, 0), col))
    col = lambda b: pl.BlockSpec((tm, c), lambda i: (i, b))
    vec = lambda a: jnp.tile(a.astype(F32), LANES // a.shape[0]).reshape(1, LANES)
    lane_vec = pl.BlockSpec((1, LANES), lambda i: (0, 0))
    return pl.pallas_call(
        functools.partial(_hyb_prep_body, tm=tm, seq=seq, tk=tk),
        grid=(n // tm,),
        in_specs=[col(0), col(1), col(2),
                  pl.BlockSpec((halo, c), prev(1)), pl.BlockSpec((halo, c), prev(2)),
                  col(3), col(4), col(5),
                  pl.BlockSpec((CONV_WIDTH, c), lambda i: (0, 0)), lane_vec, lane_vec],
        out_specs=[pl.BlockSpec((tm, c), lambda i: (i, 0)),
                   pl.BlockSpec((1, heads, LANES, tm), lambda i: (i // nt, 0, 0, i % nt)),
                   pl.BlockSpec((tm, 2 * c), lambda i: (i, 0)),
                   pl.BlockSpec((1, tm // tk, c, tk), lambda i: (i // nt, i % nt, 0, 0))],
        out_shape=[jax.ShapeDtypeStruct((n, c), BF16),
                   jax.ShapeDtypeStruct((batch, heads, LANES, seq), BF16),
                   jax.ShapeDtypeStruct((n, 2 * c), BF16),
                   jax.ShapeDtypeStruct((batch, seq // tk, c, tk), BF16)],
        compiler_params=_cparams(("parallel",), 48),
        name="hyb_prep")(proj, proj, proj, proj, proj, proj, proj, proj, conv_w.astype(F32),
                         vec(gq), vec(gk))


def _fox_prep_body(q_ref, k_ref, v_ref, c_ref, gq_ref, gk_ref, qt_ref, ko_ref, vt_ref, *, tk):
    bd = _block_diag_ones(FOX_D)
    gq, gk = gq_ref[...], gk_ref[...]
    lane = lax.broadcasted_iota(jnp.int32, (1, LANES), 1)
    ones_q = jnp.where((lane >= 3) & (lane < 6), 1.0, 0.0)
    ones_k = jnp.where(lane < 3, 1.0, 0.0)
    c_hi, c_mid, c_lo = (t.astype(F32) for t in _split3(c_ref[...]))
    for h in range(q_ref.shape[1] // LANES):
        cols = slice(h * LANES, (h + 1) * LANES)
        _store_transposed(qt_ref, h, 0, _group_rmsnorm(q_ref[:, cols], gq, bd, FOX_D))
        ko_ref[:, 2 * h * LANES:(2 * h + 1) * LANES] = _group_rmsnorm(k_ref[:, cols], gk, bd, FOX_D).astype(BF16)
        hi, mid, lo = c_hi[:, h:h + 1], c_mid[:, h:h + 1], c_lo[:, h:h + 1]
        split_q = jnp.where(lane == 0, hi, jnp.where(lane == 1, mid, jnp.where(lane == 2, lo, 0.0)))
        split_k = jnp.where(lane == 3, hi, jnp.where(lane == 4, mid, jnp.where(lane == 5, lo, 0.0)))
        _store_transposed(qt_ref, h, LANES, split_q + ones_q)
        ko_ref[:, (2 * h + 1) * LANES:(2 * h + 2) * LANES] = (ones_k - split_k).astype(BF16)
    _store_values_transposed(vt_ref, v_ref, tk)


def _fox_prep(proj, c, gq, gk, *, batch, seq, width, tk, tm=512):
    n = proj.shape[0]
    tm = min(tm, seq)
    nt = seq // tm
    heads = width // LANES
    vec = lambda a: a.astype(F32).reshape(1, LANES)
    lane_vec = pl.BlockSpec((1, LANES), lambda i: (0, 0))
    return pl.pallas_call(
        functools.partial(_fox_prep_body, tk=tk),
        grid=(n // tm,),
        in_specs=[pl.BlockSpec((tm, width), lambda i: (i, 0)),
                  pl.BlockSpec((tm, width), lambda i: (i, 1)),
                  pl.BlockSpec((tm, width), lambda i: (i, 2)),
                  pl.BlockSpec((tm, LANES), lambda i: (i, 0)), lane_vec, lane_vec],
        out_specs=[pl.BlockSpec((1, heads, 2 * LANES, tm), lambda i: (i // nt, 0, 0, i % nt)),
                   pl.BlockSpec((tm, 2 * width), lambda i: (i, 0)),
                   pl.BlockSpec((1, tm // tk, width, tk), lambda i: (i // nt, i % nt, 0, 0))],
        out_shape=[jax.ShapeDtypeStruct((batch, heads, 2 * LANES, seq), BF16),
                   jax.ShapeDtypeStruct((n, 2 * width), BF16),
                   jax.ShapeDtypeStruct((batch, seq // tk, width, tk), BF16)],
        compiler_params=_cparams(("parallel",), 56),
        name="fox_prep")(proj, proj, proj, c, vec(gq), vec(gk))


def _gate_cumsum_body(f_ref, b_ref, c_ref, tri_ref, carry_ref, *, tc):
    @pl.when(pl.program_id(1) == 0)
    def _():
        carry_ref[...] = jnp.zeros_like(carry_ref)
        r = lax.broadcasted_iota(jnp.int32, (tc, tc), 0)
        c = lax.broadcasted_iota(jnp.int32, (tc, tc), 1)
        tri_ref[...] = (r >= c).astype(BF16)

    f = f_ref[...] + b_ref[...]
    logf = -(jnp.maximum(-f, 0.0) + jnp.log1p(jnp.exp(-jnp.abs(f))))
    hi, mid, lo = _split3(logf)
    tri = tri_ref[...]
    c = (jnp.dot(tri, hi, preferred_element_type=F32) + jnp.dot(tri, mid, preferred_element_type=F32)
         + jnp.dot(tri, lo, preferred_element_type=F32)) + carry_ref[0:1, :]
    carry_ref[...] = jnp.broadcast_to(c[tc - 1:tc, :], carry_ref.shape)
    c_ref[...] = c * LOG2E


def _gate_cumsum(f, b_f, *, batch, seq, tc=512):
    nf = f.shape[1]
    tc = min(tc, seq)
    nt = seq // tc
    b2 = jnp.zeros((1, nf), F32).at[0, :b_f.shape[0]].set(b_f.astype(F32))
    return pl.pallas_call(
        functools.partial(_gate_cumsum_body, tc=tc),
        grid=(batch, nt),
        in_specs=[pl.BlockSpec((tc, nf), lambda b, j: (b * nt + j, 0)),
                  pl.BlockSpec((1, nf), lambda b, j: (0, 0))],
        out_specs=pl.BlockSpec((tc, nf), lambda b, j: (b * nt + j, 0)),
        out_shape=jax.ShapeDtypeStruct(f.shape, F32),
        scratch_shapes=[pltpu.VMEM((tc, tc), BF16), pltpu.VMEM((8, nf), F32)],
        compiler_params=_cparams(("arbitrary", "arbitrary"), 32),
        name="gate_cumsum")(f, b2)


def _flash_core(qt_ref, k_ref, vt_ref, s_ref, p_ref, a_ref, m_ref, l_ref, acc_ref, x_ref,
                *, tk, nm, dk):
    qi = pl.program_id(2)
    tq = 2 * tk
    cw = min(SCORE_COLS, tk)
    all_tiles = [slice(a, a + cw) for a in range(0, tq, cw)]
    lower_tiles = [t for t in all_tiles if t.start >= tk]
    key_minus_query = (lax.broadcasted_iota(jnp.int32, (ROW_BLOCK, cw), 0)
                       - lax.broadcasted_iota(jnp.int32, (ROW_BLOCK, cw), 1))

    def visible_rows(tile, key_col):
        return tk if key_col is None else min(tk, tile.start - key_col + cw)

    def scores(c, slot, tiles=all_tiles, key_col=None):
        start = pl.multiple_of(c * tk, tk)

        def keys(rows):
            return jnp.concatenate([k_ref[0, pl.ds(start, rows), m * dk:(m + 1) * dk]
                                    for m in range(nm)], axis=0)

        if key_col is None:
            span = slice(tiles[0].start, tiles[-1].stop)
            s_all = jnp.dot(keys(tk), qt_ref[0, 0, :, span], preferred_element_type=F32)
            for m in range(nm):
                for t in tiles:
                    s = s_all[m * tk:(m + 1) * tk, t.start - span.start:t.stop - span.start]
                    s_ref[slot * nm + m, :, t] = s
                    mx = s[0:8]
                    for r in range(8, tk, 8):
                        mx = jnp.maximum(mx, s[r:r + 8])
                    x_ref[slot * nm + m, :, t] = jnp.max(mx, axis=0, keepdims=True)
        else:
            for t in tiles:
                rows = visible_rows(t, key_col)
                s = jnp.dot(keys(rows), qt_ref[0, 0, :, t], preferred_element_type=F32)
                for m in range(nm):
                    s_ref[slot * nm + m, 0:rows, t] = s[m * rows:(m + 1) * rows]

    def softmax(slot, tiles=all_tiles, key_col=None):
        for m in range(nm):
            idx = slot * nm + m
            for t in tiles:
                rows = visible_rows(t, key_col)
                rel = None if key_col is None else t.start - key_col

                def block(r):
                    s = s_ref[idx, r:r + ROW_BLOCK, t]
                    if rel is not None and r + ROW_BLOCK - 1 > rel:
                        s = jnp.where(key_minus_query <= rel - r, s, -jnp.inf)
                    return s

                if rel is None or rel >= tk:
                    smax = x_ref[idx, :, t]
                else:
                    mx = block(0)
                    for r in range(ROW_BLOCK, rows, ROW_BLOCK):
                        mx = jnp.maximum(mx, block(r))
                    smax = jnp.max(mx, axis=0, keepdims=True)
                m_old = m_ref[m, :, t]
                m_new = jnp.maximum(m_old, smax)
                alpha = jnp.exp2(m_old - m_new)
                psum = None
                for r in range(0, rows, ROW_BLOCK):
                    p = jnp.exp2(block(r) - m_new)
                    part = p[0:8] + p[8:16]
                    psum = part if psum is None else psum + part
                    p_ref[idx, r:r + ROW_BLOCK, t] = p.astype(BF16)
                m_ref[m, :, t] = m_new
                l_ref[m, :, t] = alpha * l_ref[m, :, t] + jnp.sum(psum, axis=0, keepdims=True)
                a_ref[idx, :, t] = alpha

    def weighted_values(c, slot, tiles=all_tiles, key_col=None):
        for m in range(nm):
            idx = slot * nm + m
            for t in tiles:
                rows = visible_rows(t, key_col)
                acc_ref[m, :, t] = (a_ref[idx, :, t] * acc_ref[m, :, t]
                                    + jnp.dot(vt_ref[0, c, :, 0:rows], p_ref[idx, 0:rows, t],
                                              preferred_element_type=F32))

    m_ref[...] = jnp.full(m_ref.shape, -jnp.inf, F32)
    l_ref[...] = jnp.zeros(l_ref.shape, F32)
    acc_ref[...] = jnp.zeros(acc_ref.shape, F32)
    for m in range(nm):
        p_ref[nm + m] = jnp.zeros(p_ref.shape[1:], BF16)
        a_ref[nm + m] = jnp.ones(a_ref.shape[1:], F32)

    n_vis = 2 * qi
    scores(0, 0)

    def pair(u, carry):
        c = 2 * u
        scores(c + 1, 1)
        softmax(0)
        weighted_values(jnp.maximum(c - 1, 0), 1)
        scores(c + 2, 0)
        softmax(1)
        weighted_values(c, 0)
        return carry

    lax.fori_loop(0, qi, pair, 0)
    scores(n_vis + 1, 1, lower_tiles, key_col=tk)
    softmax(0, key_col=0)
    weighted_values(jnp.maximum(n_vis - 1, 0), 1)
    softmax(1, lower_tiles, key_col=tk)
    weighted_values(n_vis, 0, key_col=0)
    weighted_values(n_vis + 1, 1, lower_tiles, key_col=tk)


def _flash_scratch(tq, tk, nm, dv):
    return [pltpu.VMEM((2 * nm, tk, tq), F32),
            pltpu.VMEM((2 * nm, tk, tq), BF16),
            pltpu.VMEM((2 * nm, 1, tq), F32),
            pltpu.VMEM((nm, 1, tq), F32),
            pltpu.VMEM((nm, 1, tq), F32),
            pltpu.VMEM((nm, dv, tq), F32),
            pltpu.VMEM((2 * nm, 1, tq), F32)]


def _fox_attn_body(qt_ref, k_ref, vt_ref, o_ref, *scratch, tk):
    _flash_core(qt_ref, k_ref, vt_ref, *scratch, tk=tk, nm=1, dk=k_ref.shape[2])
    l_ref, acc_ref = scratch[4], scratch[5]
    o_ref[0] = (acc_ref[0] / l_ref[0]).T.astype(o_ref.dtype)


def _fox_attn(qt, ka, vt, *, tk):
    b, heads, dq, t = qt.shape
    tq = 2 * tk
    return pl.pallas_call(
        functools.partial(_fox_attn_body, tk=tk),
        grid=(b, heads, t // tq),
        in_specs=[pl.BlockSpec((1, 1, dq, tq), lambda bi, h, i: (bi, h, 0, i)),
                  pl.BlockSpec((1, t, dq), lambda bi, h, i: (bi, 0, h)),
                  pl.BlockSpec((1, t // tk, FOX_D, tk), lambda bi, h, i: (bi, 0, h, 0))],
        out_specs=pl.BlockSpec((1, tq, FOX_D), lambda bi, h, i: (bi, i, h)),
        out_shape=jax.ShapeDtypeStruct((b, t, heads * FOX_D), BF16),
        scratch_shapes=_flash_scratch(tq, tk, 1, FOX_D),
        compiler_params=_cparams(("parallel", "parallel", "arbitrary"), 48),
        name="fox_attn")(qt, ka, vt)


def _diff_attn_body(lq1_ref, lk1_ref, lq2_ref, lk2_ref, sg_ref, qt_ref, k_ref, vt_ref, o_ref,
                    *scratch, tk, lam_init):
    _flash_core(qt_ref, k_ref, vt_ref, *scratch, tk=tk, nm=2, dk=qt_ref.shape[2])
    l_ref, acc_ref = scratch[4], scratch[5]
    lam = (jnp.exp(jnp.sum(lq1_ref[...] * lk1_ref[...], keepdims=True))
           - jnp.exp(jnp.sum(lq2_ref[...] * lk2_ref[...], keepdims=True)) + lam_init)
    o = acc_ref[0] / l_ref[0] - lam * (acc_ref[1] / l_ref[1])
    ms = jnp.mean(o * o, axis=0, keepdims=True)
    o = ((o * lax.rsqrt(ms + EPS)) * sg_ref[...]) * (1.0 - lam_init)
    o_ref[0] = o.T.astype(o_ref.dtype)


def _diff_attn(qt, kn, vt, lq1, lk1, lq2, lk2, subln_g, *, lam_init, tk):
    b, heads, dq, t = qt.shape
    tq = 2 * tk
    dv = 2 * DIFF_D
    vec = lambda a: a.reshape(1, -1).astype(F32)
    small = lambda w: pl.BlockSpec((1, w), lambda bi, h, i: (0, 0))
    return pl.pallas_call(
        functools.partial(_diff_attn_body, tk=tk, lam_init=lam_init),
        grid=(b, heads, t // tq),
        in_specs=[small(DIFF_D), small(DIFF_D), small(DIFF_D), small(DIFF_D),
                  pl.BlockSpec((dv, 1), lambda bi, h, i: (0, 0)),
                  pl.BlockSpec((1, 1, dq, tq), lambda bi, h, i: (bi, h, 0, i)),
                  pl.BlockSpec((1, t, 2 * dq), lambda bi, h, i: (bi, 0, h)),
                  pl.BlockSpec((1, t // tk, dv, tk), lambda bi, h, i: (bi, 0, h, 0))],
        out_specs=pl.BlockSpec((1, tq, dv), lambda bi, h, i: (bi, i, h)),
        out_shape=jax.ShapeDtypeStruct((b, t, heads * dv), BF16),
        scratch_shapes=_flash_scratch(tq, tk, 2, dv),
        compiler_params=_cparams(("parallel", "parallel", "arbitrary"), 48),
        name="diff_attn")(vec(lq1), vec(lk1), vec(lq2), vec(lk2),
                          subln_g.reshape(dv, 1).astype(F32), qt, kn, vt)


def kernel(x, norm1_g, norm2_g, hyb_w_in, hyb_conv_w, hyb_dq_g, hyb_dk_g, hyb_lq1, hyb_lk1,
           hyb_lq2, hyb_lk2, hyb_subln_g, hyb_w_out, fox_w_in, fox_b_f, fox_q_g, fox_k_g,
           fox_w_out, mlp_w1, mlp_w2):
    b, t, d = x.shape
    n = b * t
    depth = norm1_g.shape[0]
    diff_tk = fox_tk = min(ATTN_KEY_CHUNK, t // 2)
    xf = x.reshape(n, d)
    for l in range(depth):
        j = l // 2
        if l % 2 == 0:
            proj = _norm_mm(xf, norm1_g[l], _to_bf16(hyb_w_in, j))
            gq = hyb_dq_g[j].astype(F32) * (DIFF_D ** -0.5 * LOG2E)
            y_conv, qt, kn, vt = _hyb_prep(proj, hyb_conv_w[j], gq, hyb_dk_g[j], batch=b, seq=t,
                                           tk=diff_tk)
            att = _diff_attn(qt, kn.reshape(b, t, -1), vt, hyb_lq1[j], hyb_lk1[j], hyb_lq2[j],
                             hyb_lk2[j], hyb_subln_g[j], lam_init=0.8 - 0.6 * math.exp(-0.3 * l),
                             tk=diff_tk)
            xf = _mm2_res(y_conv, att.reshape(n, -1), _to_bf16(hyb_w_out, j), xf)
        else:
            w_in = _to_bf16(fox_w_in, j)
            wf = jnp.zeros((d, LANES), BF16).at[:, :FOX_HEADS].set(w_in[:, 3 * d:])
            proj, f = _norm_mm(xf, norm1_g[l], w_in, nout=3 * d, wf=wf)
            c = _gate_cumsum(f, fox_b_f[j], batch=b, seq=t)
            gq = fox_q_g[j].astype(F32) * (FOX_D ** -0.5 * LOG2E)
            qt, ka, vt = _fox_prep(proj, c, gq, fox_k_g[j], batch=b, seq=t, width=d, tk=fox_tk)
            att = _fox_attn(qt, ka.reshape(b, t, -1), vt, tk=fox_tk).reshape(n, -1)
            xf = _mm2_res(att, att, _to_bf16(fox_w_out, j), xf, a2_block=1)
        hid = _norm_mm(xf, norm2_g[l], _to_bf16(mlp_w1, l), sqrelu=True)
        xf = _mm_res(hid, _to_bf16(mlp_w2, l), xf)
    return xf.reshape(b, t, d)
```

```python
import functools
import math

import jax
import jax.numpy as jnp
from jax import lax
from jax.experimental import pallas as pl
from jax.experimental.pallas import tpu as pltpu

F32 = jnp.float32
BF16 = jnp.bfloat16
EPS = 1e-6
LOG2E = math.log2(math.e)
LANES = 128
BF16_SUBLANES = 16
MIB = 2 ** 20

CONV_WIDTH = 3
DIFF_HEADS = 8
DIFF_D = 64
FOX_HEADS = 16
FOX_D = 128

ATTN_KEY_CHUNK = 512
SCORE_COLS = 256
ROW_BLOCK = BF16_SUBLANES


def _cparams(semantics, vmem_mib):
    return pltpu.CompilerParams(dimension_semantics=semantics, vmem_limit_bytes=vmem_mib * MIB)


def _split3(x):
    hi = x.astype(BF16)
    r1 = x - hi.astype(F32)
    mid = r1.astype(BF16)
    lo = (r1 - mid.astype(F32)).astype(BF16)
    return hi, mid, lo


CAST_BLOCK_ELEMS = 2 ** 21


def _cast_body(w_ref, o_ref):
    o_ref[...] = w_ref[0].astype(o_ref.dtype)


def _to_bf16(w_stack, layer):
    _, rows, cols = w_stack.shape
    tr = min(rows, max(BF16_SUBLANES, CAST_BLOCK_ELEMS // cols // BF16_SUBLANES * BF16_SUBLANES))
    while rows % tr:
        tr -= BF16_SUBLANES
    return pl.pallas_call(
        _cast_body,
        grid=(rows // tr,),
        in_specs=[pl.BlockSpec((1, tr, cols), lambda i: (layer, i, 0))],
        out_specs=pl.BlockSpec((tr, cols), lambda i: (i, 0)),
        out_shape=jax.ShapeDtypeStruct((rows, cols), BF16),
        compiler_params=_cparams(("parallel",), 40),
        name="to_bf16")(w_stack)


def _normalize_rows(x_ref, g_ref, h_ref):
    x = x_ref[...]
    ms = jnp.mean(x * x, axis=-1, keepdims=True)
    h_ref[...] = ((x * lax.rsqrt(ms + EPS)) * g_ref[...]).astype(BF16)


def _norm_mm_body(x_ref, g_ref, w_ref, o_ref, h_ref, *, sqrelu):
    @pl.when(pl.program_id(1) == 0)
    def _():
        _normalize_rows(x_ref, g_ref, h_ref)

    y = jnp.dot(h_ref[...], w_ref[...], preferred_element_type=F32)
    if sqrelu:
        y = jnp.square(jnp.maximum(y, 0.0))
    o_ref[...] = y.astype(o_ref.dtype)


def _norm_mm_gate_body(x_ref, g_ref, w_ref, wf_ref, o_ref, f_ref, h_ref):
    @pl.when(pl.program_id(1) == 0)
    def _():
        _normalize_rows(x_ref, g_ref, h_ref)
        f_ref[...] = jnp.dot(h_ref[...], wf_ref[...], preferred_element_type=F32)

    y = jnp.dot(h_ref[...], w_ref[...], preferred_element_type=F32)
    o_ref[...] = y.astype(o_ref.dtype)


def _norm_mm(x, g, w, *, nout=None, sqrelu=False, wf=None, tm=1024, tn=1024):
    n, d = x.shape
    nout = w.shape[1] if nout is None else nout
    tm, tn = min(tm, n), min(tn, nout)
    grid = (n // tm, nout // tn)
    x_spec = pl.BlockSpec((tm, d), lambda i, j: (i, 0))
    g_spec = pl.BlockSpec((1, d), lambda i, j: (0, 0))
    w_spec = pl.BlockSpec((d, tn), lambda i, j: (0, j))
    o_spec = pl.BlockSpec((tm, tn), lambda i, j: (i, j))
    scratch = [pltpu.VMEM((tm, d), BF16)]
    params = _cparams(("parallel", "arbitrary"), 48)
    g2 = g.reshape(1, d).astype(F32)
    if wf is None:
        return pl.pallas_call(
            functools.partial(_norm_mm_body, sqrelu=sqrelu),
            grid=grid, in_specs=[x_spec, g_spec, w_spec], out_specs=o_spec,
            out_shape=jax.ShapeDtypeStruct((n, nout), BF16),
            scratch_shapes=scratch, compiler_params=params, name="norm_mm")(x, g2, w)
    nf = wf.shape[1]
    return pl.pallas_call(
        _norm_mm_gate_body,
        grid=grid,
        in_specs=[x_spec, g_spec, w_spec, pl.BlockSpec((d, nf), lambda i, j: (0, 0))],
        out_specs=[o_spec, pl.BlockSpec((tm, nf), lambda i, j: (i, 0))],
        out_shape=[jax.ShapeDtypeStruct((n, nout), BF16), jax.ShapeDtypeStruct((n, nf), F32)],
        scratch_shapes=scratch, compiler_params=params, name="norm_mm_gate")(x, g2, w, wf)


def _mm_res_body(a_ref, w_ref, r_ref, o_ref):
    d = jnp.dot(a_ref[...], w_ref[...], preferred_element_type=F32)

    @pl.when(pl.program_id(2) == 0)
    def _():
        o_ref[...] = r_ref[...] + d

    @pl.when(pl.program_id(2) != 0)
    def _():
        o_ref[...] += d


def _mm_res(a, w, r, *, tm=1024, tn=1024, tk=2048):
    n, kdim = a.shape
    nout = w.shape[1]
    tm, tn, tk = min(tm, n), min(tn, nout), min(tk, kdim)
    return pl.pallas_call(
        _mm_res_body,
        grid=(n // tm, nout // tn, kdim // tk),
        in_specs=[pl.BlockSpec((tm, tk), lambda i, j, k: (i, k)),
                  pl.BlockSpec((tk, tn), lambda i, j, k: (k, j)),
                  pl.BlockSpec((tm, tn), lambda i, j, k: (i, j))],
        out_specs=pl.BlockSpec((tm, tn), lambda i, j, k: (i, j)),
        out_shape=jax.ShapeDtypeStruct((n, nout), F32),
        compiler_params=_cparams(("parallel", "parallel", "arbitrary"), 48),
        name="mm_res")(a, w, r)


def _mm2_res_body(a1_ref, a2_ref, w_ref, r_ref, o_ref):
    k1 = a1_ref.shape[1]
    o_ref[...] = (r_ref[...]
                  + jnp.dot(a1_ref[...], w_ref[:k1, :], preferred_element_type=F32)
                  + jnp.dot(a2_ref[...], w_ref[k1:, :], preferred_element_type=F32))


def _mm2_res(a1, a2, w, r, *, a2_block=0, tm=512):
    n = a1.shape[0]
    kdim, nout = w.shape
    k1 = k2 = kdim // 2
    tm = min(tm, n)
    return pl.pallas_call(
        _mm2_res_body,
        grid=(n // tm,),
        in_specs=[pl.BlockSpec((tm, k1), lambda i: (i, 0)),
                  pl.BlockSpec((tm, k2), lambda i: (i, a2_block)),
                  pl.BlockSpec((k1 + k2, nout), lambda i: (0, 0)),
                  pl.BlockSpec((tm, nout), lambda i: (i, 0))],
        out_specs=pl.BlockSpec((tm, nout), lambda i: (i, 0)),
        out_shape=jax.ShapeDtypeStruct((n, nout), F32),
        compiler_params=_cparams(("parallel",), 48),
        name="mm2_res")(a1, a2, w, r)


def _block_diag_ones(group):
    r = (lax.broadcasted_iota(jnp.int32, (2 * LANES, LANES), 0) % LANES) // group
    c = lax.broadcasted_iota(jnp.int32, (2 * LANES, LANES), 1) // group
    return (r == c).astype(BF16)


def _group_rmsnorm(x, gain, bd, group):
    xf = x.astype(F32)
    sq = xf * xf
    hi = sq.astype(BF16)
    lo = (sq - hi.astype(F32)).astype(BF16)
    gs = jnp.dot(jnp.concatenate([hi, lo], axis=1), bd, preferred_element_type=F32)
    return (xf * lax.rsqrt(gs * (1.0 / group) + EPS)) * gain


def _store_transposed(dst_ref, head, row0, x):
    dst_ref[0, head, row0:row0 + LANES, :] = x.astype(dst_ref.dtype).T


def _store_values_transposed(vt_ref, v_ref, tk):
    tm = v_ref.shape[0]
    for h in range(v_ref.shape[1] // LANES):
        vt = v_ref[:, h * LANES:(h + 1) * LANES].T
        for ch in range(tm // tk):
            vt_ref[0, ch, h * LANES:(h + 1) * LANES, :] = vt[:, ch * tk:(ch + 1) * tk]


def _hyb_prep_body(gb_ref, gc_ref, u_ref, gcp_ref, up_ref, q_ref, k_ref, v_ref, cw_ref, gq_ref,
                   gk_ref, y_ref, qt_ref, ko_ref, vt_ref, *, tm, seq, tk):
    z = gc_ref[...].astype(F32) * u_ref[...].astype(F32)
    zp = gcp_ref[...].astype(F32) * up_ref[...].astype(F32)
    first_of_seq = lax.rem(pl.program_id(0) * tm, seq) == 0
    zp = jnp.where(first_of_seq, 0.0, zp)
    zext = jnp.concatenate([zp, z], axis=0)
    z1 = pltpu.roll(zext, 1, 0)[BF16_SUBLANES:]
    z2 = pltpu.roll(zext, 2, 0)[BF16_SUBLANES:]
    cw = cw_ref[...]
    y = gb_ref[...].astype(F32) * (cw[0:1] * z2 + cw[1:2] * z1 + cw[2:3] * z)
    y_ref[...] = y.astype(BF16)

    bd = _block_diag_ones(DIFF_D)
    gq, gk = gq_ref[...], gk_ref[...]
    low_half = lax.broadcasted_iota(jnp.int32, (1, LANES), 1) < DIFF_D
    for h in range(q_ref.shape[1] // LANES):
        cols = slice(h * LANES, (h + 1) * LANES)
        _store_transposed(qt_ref, h, 0, _group_rmsnorm(q_ref[:, cols], gq, bd, DIFF_D))
        kn = _group_rmsnorm(k_ref[:, cols], gk, bd, DIFF_D)
        ko_ref[:, 2 * h * LANES:(2 * h + 1) * LANES] = jnp.where(low_half, kn, 0.0).astype(BF16)
        ko_ref[:, (2 * h + 1) * LANES:(2 * h + 2) * LANES] = jnp.where(low_half, 0.0, kn).astype(BF16)
    _store_values_transposed(vt_ref, v_ref, tk)


def _hyb_prep(proj, conv_w, gq, gk, *, batch, seq, tk, tm=512):
    n = proj.shape[0]
    c = conv_w.shape[1]
    tm = min(tm, seq)
    nt = seq // tm
    heads = c // LANES
    halo = BF16_SUBLANES
    prev = lambda col: (lambda i: (jnp.maximum(i * (tm // halo) - 1, 0), col))
    col = lambda b: pl.BlockSpec((tm, c), lambda i: (i, b))
    vec = lambda a: jnp.tile(a.astype(F32), LANES // a.shape[0]).reshape(1, LANES)
    lane_vec = pl.BlockSpec((1, LANES), lambda i: (0, 0))
    return pl.pallas_call(
        functools.partial(_hyb_prep_body, tm=tm, seq=seq, tk=tk),
        grid=(n // tm,),
        in_specs=[col(0), col(1), col(2),
                  pl.BlockSpec((halo, c), prev(1)), pl.BlockSpec((halo, c), prev(2)),
                  col(3), col(4), col(5),
                  pl.BlockSpec((CONV_WIDTH, c), lambda i: (0, 0)), lane_vec, lane_vec],
        out_specs=[pl.BlockSpec((tm, c), lambda i: (i, 0)),
                   pl.BlockSpec((1, heads, LANES, tm), lambda i: (i // nt, 0, 0, i % nt)),
                   pl.BlockSpec((tm, 2 * c), lambda i: (i, 0)),
                   pl.BlockSpec((1, tm // tk, c, tk), lambda i: (i // nt, i % nt, 0, 0))],
        out_shape=[jax.ShapeDtypeStruct((n, c), BF16),
                   jax.ShapeDtypeStruct((batch, heads, LANES, seq), BF16),
                   jax.ShapeDtypeStruct((n, 2 * c), BF16),
                   jax.ShapeDtypeStruct((batch, seq // tk, c, tk), BF16)],
        compiler_params=_cparams(("parallel",), 48),
        name="hyb_prep")(proj, proj, proj, proj, proj, proj, proj, proj, conv_w.astype(F32),
                         vec(gq), vec(gk))


def _fox_prep_body(q_ref, k_ref, v_ref, c_ref, gq_ref, gk_ref, qt_ref, ko_ref, vt_ref, *, tk):
    bd = _block_diag_ones(FOX_D)
    gq, gk = gq_ref[...], gk_ref[...]
    lane = lax.broadcasted_iota(jnp.int32, (1, LANES), 1)
    ones_q = jnp.where((lane >= 3) & (lane < 6), 1.0, 0.0)
    ones_k = jnp.where(lane < 3, 1.0, 0.0)
    c_hi, c_mid, c_lo = (t.astype(F32) for t in _split3(c_ref[...]))
    for h in range(q_ref.shape[1] // LANES):
        cols = slice(h * LANES, (h + 1) * LANES)
        _store_transposed(qt_ref, h, 0, _group_rmsnorm(q_ref[:, cols], gq, bd, FOX_D))
        ko_ref[:, 2 * h * LANES:(2 * h + 1) * LANES] = _group_rmsnorm(k_ref[:, cols], gk, bd, FOX_D).astype(BF16)
        hi, mid, lo = c_hi[:, h:h + 1], c_mid[:, h:h + 1], c_lo[:, h:h + 1]
        split_q = jnp.where(lane == 0, hi, jnp.where(lane == 1, mid, jnp.where(lane == 2, lo, 0.0)))
        split_k = jnp.where(lane == 3, hi, jnp.where(lane == 4, mid, jnp.where(lane == 5, lo, 0.0)))
        _store_transposed(qt_ref, h, LANES, split_q + ones_q)
        ko_ref[:, (2 * h + 1) * LANES:(2 * h + 2) * LANES] = (ones_k - split_k).astype(BF16)
    _store_values_transposed(vt_ref, v_ref, tk)


def _fox_prep(proj, c, gq, gk, *, batch, seq, width, tk, tm=512):
    n = proj.shape[0]
    tm = min(tm, seq)
    nt = seq // tm
    heads = width // LANES
    vec = lambda a: a.astype(F32).reshape(1, LANES)
    lane_vec = pl.BlockSpec((1, LANES), lambda i: (0, 0))
    return pl.pallas_call(
        functools.partial(_fox_prep_body, tk=tk),
        grid=(n // tm,),
        in_specs=[pl.BlockSpec((tm, width), lambda i: (i, 0)),
                  pl.BlockSpec((tm, width), lambda i: (i, 1)),
                  pl.BlockSpec((tm, width), lambda i: (i, 2)),
                  pl.BlockSpec((tm, LANES), lambda i: (i, 0)), lane_vec, lane_vec],
        out_specs=[pl.BlockSpec((1, heads, 2 * LANES, tm), lambda i: (i // nt, 0, 0, i % nt)),
                   pl.BlockSpec((tm, 2 * width), lambda i: (i, 0)),
                   pl.BlockSpec((1, tm // tk, width, tk), lambda i: (i // nt, i % nt, 0, 0))],
        out_shape=[jax.ShapeDtypeStruct((batch, heads, 2 * LANES, seq), BF16),
                   jax.ShapeDtypeStruct((n, 2 * width), BF16),
                   jax.ShapeDtypeStruct((batch, seq // tk, width, tk), BF16)],
        compiler_params=_cparams(("parallel",), 56),
        name="fox_prep")(proj, proj, proj, c, vec(gq), vec(gk))


def _gate_cumsum_body(f_ref, b_ref, c_ref, tri_ref, carry_ref, *, tc):
    @pl.when(pl.program_id(1) == 0)
    def _():
        carry_ref[...] = jnp.zeros_like(carry_ref)
        r = lax.broadcasted_iota(jnp.int32, (tc, tc), 0)
        c = lax.broadcasted_iota(jnp.int32, (tc, tc), 1)
        tri_ref[...] = (r >= c).astype(BF16)

    f = f_ref[...] + b_ref[...]
    logf = -(jnp.maximum(-f, 0.0) + jnp.log1p(jnp.exp(-jnp.abs(f))))
    hi, mid, lo = _split3(logf)
    tri = tri_ref[...]
    c = (jnp.dot(tri, hi, preferred_element_type=F32) + jnp.dot(tri, mid, preferred_element_type=F32)
         + jnp.dot(tri, lo, preferred_element_type=F32)) + carry_ref[0:1, :]
    carry_ref[...] = jnp.broadcast_to(c[tc - 1:tc, :], carry_ref.shape)
    c_ref[...] = c * LOG2E


def _gate_cumsum(f, b_f, *, batch, seq, tc=512):
    nf = f.shape[1]
    tc = min(tc, seq)
    nt = seq // tc
    b2 = jnp.zeros((1, nf), F32).at[0, :b_f.shape[0]].set(b_f.astype(F32))
    return pl.pallas_call(
        functools.partial(_gate_cumsum_body, tc=tc),
        grid=(batch, nt),
        in_specs=[pl.BlockSpec((tc, nf), lambda b, j: (b * nt + j, 0)),
                  pl.BlockSpec((1, nf), lambda b, j: (0, 0))],
        out_specs=pl.BlockSpec((tc, nf), lambda b, j: (b * nt + j, 0)),
        out_shape=jax.ShapeDtypeStruct(f.shape, F32),
        scratch_shapes=[pltpu.VMEM((tc, tc), BF16), pltpu.VMEM((8, nf), F32)],
        compiler_params=_cparams(("arbitrary", "arbitrary"), 32),
        name="gate_cumsum")(f, b2)


def _flash_core(qt_ref, k_ref, vt_ref, s_ref, p_ref, a_ref, m_ref, l_ref, acc_ref, x_ref,
                *, tk, nm, dk, shared_qv):
    qi = pl.program_id(2)
    tq = 2 * tk
    dv = acc_ref.shape[1]
    cw = min(SCORE_COLS, tk)
    all_tiles = [slice(a, a + cw) for a in range(0, tq, cw)]
    lower_tiles = [t for t in all_tiles if t.start >= tk]
    key_minus_query = (lax.broadcasted_iota(jnp.int32, (ROW_BLOCK, cw), 0)
                       - lax.broadcasted_iota(jnp.int32, (ROW_BLOCK, cw), 1))

    def visible_rows(tile, key_col):
        return tk if key_col is None else min(tk, tile.start - key_col + cw)

    def scores(c, slot, tiles=all_tiles, key_col=None):
        start = pl.multiple_of(c * tk, tk)

        def map_scores(rows, cols):
            ks = [k_ref[0, pl.ds(start, rows), m * dk:(m + 1) * dk] for m in range(nm)]
            if shared_qv:
                s = jnp.dot(jnp.concatenate(ks, axis=0), qt_ref[0, 0, :, cols],
                            preferred_element_type=F32)
                return [s[m * rows:(m + 1) * rows] for m in range(nm)]
            return [jnp.dot(ks[m], qt_ref[0, m, :, cols], preferred_element_type=F32)
                    for m in range(nm)]

        if key_col is None:
            span = slice(tiles[0].start, tiles[-1].stop)
            for m, s_all in enumerate(map_scores(tk, span)):
                for t in tiles:
                    s = s_all[:, t.start - span.start:t.stop - span.start]
                    s_ref[slot * nm + m, :, t] = s
                    mx = s[0:8]
                    for r in range(8, tk, 8):
                        mx = jnp.maximum(mx, s[r:r + 8])
                    x_ref[slot * nm + m, :, t] = jnp.max(mx, axis=0, keepdims=True)
        else:
            for t in tiles:
                rows = visible_rows(t, key_col)
                for m, s in enumerate(map_scores(rows, t)):
                    s_ref[slot * nm + m, 0:rows, t] = s

    def softmax(slot, tiles=all_tiles, key_col=None):
        for m in range(nm):
            idx = slot * nm + m
            for t in tiles:
                rows = visible_rows(t, key_col)
                rel = None if key_col is None else t.start - key_col

                def block(r):
                    s = s_ref[idx, r:r + ROW_BLOCK, t]
                    if rel is not None and r + ROW_BLOCK - 1 > rel:
                        s = jnp.where(key_minus_query <= rel - r, s, -jnp.inf)
                    return s

                if rel is None or rel >= tk:
                    smax = x_ref[idx, :, t]
                else:
                    mx = block(0)
                    for r in range(ROW_BLOCK, rows, ROW_BLOCK):
                        mx = jnp.maximum(mx, block(r))
                    smax = jnp.max(mx, axis=0, keepdims=True)
                m_old = m_ref[m, :, t]
                m_new = jnp.maximum(m_old, smax)
                alpha = jnp.exp2(m_old - m_new)
                psum = None
                for r in range(0, rows, ROW_BLOCK):
                    p = jnp.exp2(block(r) - m_new)
                    part = p[0:8] + p[8:16]
                    psum = part if psum is None else psum + part
                    p_ref[idx, r:r + ROW_BLOCK, t] = p.astype(BF16)
                m_ref[m, :, t] = m_new
                l_ref[m, :, t] = alpha * l_ref[m, :, t] + jnp.sum(psum, axis=0, keepdims=True)
                a_ref[idx, :, t] = alpha

    def weighted_values(c, slot, tiles=all_tiles, key_col=None):
        for m in range(nm):
            idx = slot * nm + m
            for t in tiles:
                rows = visible_rows(t, key_col)
                feat = slice(0, dv) if shared_qv else slice(m * dv, (m + 1) * dv)
                acc_ref[m, :, t] = (a_ref[idx, :, t] * acc_ref[m, :, t]
                                    + jnp.dot(vt_ref[0, c, feat, 0:rows], p_ref[idx, 0:rows, t],
                                              preferred_element_type=F32))

    m_ref[...] = jnp.full(m_ref.shape, -jnp.inf, F32)
    l_ref[...] = jnp.zeros(l_ref.shape, F32)
    acc_ref[...] = jnp.zeros(acc_ref.shape, F32)
    for m in range(nm):
        p_ref[nm + m] = jnp.zeros(p_ref.shape[1:], BF16)
        a_ref[nm + m] = jnp.ones(a_ref.shape[1:], F32)

    n_vis = 2 * qi
    scores(0, 0)

    def pair(u, carry):
        c = 2 * u
        scores(c + 1, 1)
        softmax(0)
        weighted_values(jnp.maximum(c - 1, 0), 1)
        scores(c + 2, 0)
        softmax(1)
        weighted_values(c, 0)
        return carry

    lax.fori_loop(0, qi, pair, 0)
    scores(n_vis + 1, 1, lower_tiles, key_col=tk)
    softmax(0, key_col=0)
    weighted_values(jnp.maximum(n_vis - 1, 0), 1)
    softmax(1, lower_tiles, key_col=tk)
    weighted_values(n_vis, 0, key_col=0)
    weighted_values(n_vis + 1, 1, lower_tiles, key_col=tk)


def _flash_scratch(tq, tk, nm, dv):
    return [pltpu.VMEM((2 * nm, tk, tq), F32),
            pltpu.VMEM((2 * nm, tk, tq), BF16),
            pltpu.VMEM((2 * nm, 1, tq), F32),
            pltpu.VMEM((nm, 1, tq), F32),
            pltpu.VMEM((nm, 1, tq), F32),
            pltpu.VMEM((nm, dv, tq), F32),
            pltpu.VMEM((2 * nm, 1, tq), F32)]


def _fox_attn_body(qt_ref, k_ref, vt_ref, o_ref, *scratch, tk, heads):
    _flash_core(qt_ref, k_ref, vt_ref, *scratch, tk=tk, nm=heads, dk=k_ref.shape[2] // heads,
                shared_qv=False)
    l_ref, acc_ref = scratch[4], scratch[5]
    for m in range(heads):
        o_ref[0, :, m * FOX_D:(m + 1) * FOX_D] = (acc_ref[m] / l_ref[m]).T.astype(o_ref.dtype)


def _fox_attn(qt, ka, vt, *, tk, heads_per_step=2):
    b, heads, dq, t = qt.shape
    tq = 2 * tk
    hs = heads_per_step
    return pl.pallas_call(
        functools.partial(_fox_attn_body, tk=tk, heads=hs),
        grid=(b, heads // hs, t // tq),
        in_specs=[pl.BlockSpec((1, hs, dq, tq), lambda bi, h, i: (bi, h, 0, i)),
                  pl.BlockSpec((1, t, hs * dq), lambda bi, h, i: (bi, 0, h)),
                  pl.BlockSpec((1, t // tk, hs * FOX_D, tk), lambda bi, h, i: (bi, 0, h, 0))],
        out_specs=pl.BlockSpec((1, tq, hs * FOX_D), lambda bi, h, i: (bi, i, h)),
        out_shape=jax.ShapeDtypeStruct((b, t, heads * FOX_D), BF16),
        scratch_shapes=_flash_scratch(tq, tk, hs, FOX_D),
        compiler_params=_cparams(("parallel", "parallel", "arbitrary"), 56),
        name="fox_attn")(qt, ka, vt)


def _diff_attn_body(lq1_ref, lk1_ref, lq2_ref, lk2_ref, sg_ref, qt_ref, k_ref, vt_ref, o_ref,
                    *scratch, tk, lam_init):
    _flash_core(qt_ref, k_ref, vt_ref, *scratch, tk=tk, nm=2, dk=qt_ref.shape[2], shared_qv=True)
    l_ref, acc_ref = scratch[4], scratch[5]
    lam = (jnp.exp(jnp.sum(lq1_ref[...] * lk1_ref[...], keepdims=True))
           - jnp.exp(jnp.sum(lq2_ref[...] * lk2_ref[...], keepdims=True)) + lam_init)
    o = acc_ref[0] / l_ref[0] - lam * (acc_ref[1] / l_ref[1])
    ms = jnp.mean(o * o, axis=0, keepdims=True)
    o = ((o * lax.rsqrt(ms + EPS)) * sg_ref[...]) * (1.0 - lam_init)
    o_ref[0] = o.T.astype(o_ref.dtype)


def _diff_attn(qt, kn, vt, lq1, lk1, lq2, lk2, subln_g, *, lam_init, tk):
    b, heads, dq, t = qt.shape
    tq = 2 * tk
    dv = 2 * DIFF_D
    vec = lambda a: a.reshape(1, -1).astype(F32)
    small = lambda w: pl.BlockSpec((1, w), lambda bi, h, i: (0, 0))
    return pl.pallas_call(
        functools.partial(_diff_attn_body, tk=tk, lam_init=lam_init),
        grid=(b, heads, t // tq),
        in_specs=[small(DIFF_D), small(DIFF_D), small(DIFF_D), small(DIFF_D),
                  pl.BlockSpec((dv, 1), lambda bi, h, i: (0, 0)),
                  pl.BlockSpec((1, 1, dq, tq), lambda bi, h, i: (bi, h, 0, i)),
                  pl.BlockSpec((1, t, 2 * dq), lambda bi, h, i: (bi, 0, h)),
                  pl.BlockSpec((1, t // tk, dv, tk), lambda bi, h, i: (bi, 0, h, 0))],
        out_specs=pl.BlockSpec((1, tq, dv), lambda bi, h, i: (bi, i, h)),
        out_shape=jax.ShapeDtypeStruct((b, t, heads * dv), BF16),
        scratch_shapes=_flash_scratch(tq, tk, 2, dv),
        compiler_params=_cparams(("parallel", "parallel", "arbitrary"), 48),
        name="diff_attn")(vec(lq1), vec(lk1), vec(lq2), vec(lk2),
                          subln_g.reshape(dv, 1).astype(F32), qt, kn, vt)


def kernel(x, norm1_g, norm2_g, hyb_w_in, hyb_conv_w, hyb_dq_g, hyb_dk_g, hyb_lq1, hyb_lk1,
           hyb_lq2, hyb_lk2, hyb_subln_g, hyb_w_out, fox_w_in, fox_b_f, fox_q_g, fox_k_g,
           fox_w_out, mlp_w1, mlp_w2):
    b, t, d = x.shape
    n = b * t
    depth = norm1_g.shape[0]
    diff_tk = fox_tk = min(ATTN_KEY_CHUNK, t // 2)
    xf = x.reshape(n, d)
    for l in range(depth):
        j = l // 2
        if l % 2 == 0:
            proj = _norm_mm(xf, norm1_g[l], _to_bf16(hyb_w_in, j))
            gq = hyb_dq_g[j].astype(F32) * (DIFF_D ** -0.5 * LOG2E)
            y_conv, qt, kn, vt = _hyb_prep(proj, hyb_conv_w[j], gq, hyb_dk_g[j], batch=b, seq=t,
                                           tk=diff_tk)
            att = _diff_attn(qt, kn.reshape(b, t, -1), vt, hyb_lq1[j], hyb_lk1[j], hyb_lq2[j],
                             hyb_lk2[j], hyb_subln_g[j], lam_init=0.8 - 0.6 * math.exp(-0.3 * l),
                             tk=diff_tk)
            xf = _mm2_res(y_conv, att.reshape(n, -1), _to_bf16(hyb_w_out, j), xf)
        else:
            w_in = _to_bf16(fox_w_in, j)
            wf = jnp.zeros((d, LANES), BF16).at[:, :FOX_HEADS].set(w_in[:, 3 * d:])
            proj, f = _norm_mm(xf, norm1_g[l], w_in, nout=3 * d, wf=wf)
            c = _gate_cumsum(f, fox_b_f[j], batch=b, seq=t)
            gq = fox_q_g[j].astype(F32) * (FOX_D ** -0.5 * LOG2E)
            qt, ka, vt = _fox_prep(proj, c, gq, fox_k_g[j], batch=b, seq=t, width=d, tk=fox_tk)
            att = _fox_attn(qt, ka.reshape(b, t, -1), vt, tk=fox_tk).reshape(n, -1)
            xf = _mm2_res(att, att, _to_bf16(fox_w_out, j), xf, a2_block=1)
        hid = _norm_mm(xf, norm2_g[l], _to_bf16(mlp_w1, l), sqrelu=True)
        xf = _mm_res(hid, _to_bf16(mlp_w2, l), xf)
    return xf.reshape(b, t, d)
```

```python
import functools
import math

import jax
import jax.numpy as jnp
from jax import lax
from jax.experimental import pallas as pl
from jax.experimental.pallas import tpu as pltpu

F32 = jnp.float32
BF16 = jnp.bfloat16
EPS = 1e-6
LOG2E = math.log2(math.e)
LANES = 128
BF16_SUBLANES = 16
MIB = 2 ** 20

CONV_WIDTH = 3
DIFF_HEADS = 8
DIFF_D = 64
FOX_HEADS = 16
FOX_D = 128

ATTN_KEY_CHUNK = 512
SCORE_COLS = 256
ROW_BLOCK = BF16_SUBLANES


def _cparams(semantics, vmem_mib):
    return pltpu.CompilerParams(dimension_semantics=semantics, vmem_limit_bytes=vmem_mib * MIB)


def _split3(x):
    hi = x.astype(BF16)
    r1 = x - hi.astype(F32)
    mid = r1.astype(BF16)
    lo = (r1 - mid.astype(F32)).astype(BF16)
    return hi, mid, lo


CAST_BLOCK_ELEMS = 2 ** 21


def _cast_body(w_ref, o_ref):
    o_ref[...] = w_ref[0].astype(o_ref.dtype)


def _to_bf16(w_stack, layer):
    _, rows, cols = w_stack.shape
    tr = min(rows, max(BF16_SUBLANES, CAST_BLOCK_ELEMS // cols // BF16_SUBLANES * BF16_SUBLANES))
    while rows % tr:
        tr -= BF16_SUBLANES
    return pl.pallas_call(
        _cast_body,
        grid=(rows // tr,),
        in_specs=[pl.BlockSpec((1, tr, cols), lambda i: (layer, i, 0))],
        out_specs=pl.BlockSpec((tr, cols), lambda i: (i, 0)),
        out_shape=jax.ShapeDtypeStruct((rows, cols), BF16),
        compiler_params=_cparams(("parallel",), 40),
        name="to_bf16")(w_stack)


def _normalize_rows(x_ref, g_ref, h_ref):
    x = x_ref[...]
    ms = jnp.mean(x * x, axis=-1, keepdims=True)
    h_ref[...] = ((x * lax.rsqrt(ms + EPS)) * g_ref[...]).astype(BF16)


def _norm_mm_body(x_ref, g_ref, w_ref, o_ref, h_ref, *, sqrelu):
    @pl.when(pl.program_id(1) == 0)
    def _():
        _normalize_rows(x_ref, g_ref, h_ref)

    y = jnp.dot(h_ref[...], w_ref[...], preferred_element_type=F32)
    if sqrelu:
        y = jnp.square(jnp.maximum(y, 0.0))
    o_ref[...] = y.astype(o_ref.dtype)


def _norm_mm_gate_body(x_ref, g_ref, w_ref, wf_ref, o_ref, f_ref, h_ref):
    @pl.when(pl.program_id(1) == 0)
    def _():
        _normalize_rows(x_ref, g_ref, h_ref)
        f_ref[...] = jnp.dot(h_ref[...], wf_ref[...], preferred_element_type=F32)

    y = jnp.dot(h_ref[...], w_ref[...], preferred_element_type=F32)
    o_ref[...] = y.astype(o_ref.dtype)


def _norm_mm(x, g, w, *, nout=None, sqrelu=False, wf=None, tm=1024, tn=2048):
    n, d = x.shape
    nout = w.shape[1] if nout is None else nout
    tm, tn = min(tm, n), min(tn, nout)
    grid = (n // tm, nout // tn)
    x_spec = pl.BlockSpec((tm, d), lambda i, j: (i, 0))
    g_spec = pl.BlockSpec((1, d), lambda i, j: (0, 0))
    w_spec = pl.BlockSpec((d, tn), lambda i, j: (0, j))
    o_spec = pl.BlockSpec((tm, tn), lambda i, j: (i, j))
    scratch = [pltpu.VMEM((tm, d), BF16)]
    params = _cparams(("parallel", "arbitrary"), 58)
    g2 = g.reshape(1, d).astype(F32)
    if wf is None:
        return pl.pallas_call(
            functools.partial(_norm_mm_body, sqrelu=sqrelu),
            grid=grid, in_specs=[x_spec, g_spec, w_spec], out_specs=o_spec,
            out_shape=jax.ShapeDtypeStruct((n, nout), BF16),
            scratch_shapes=scratch, compiler_params=params, name="norm_mm")(x, g2, w)
    nf = wf.shape[1]
    return pl.pallas_call(
        _norm_mm_gate_body,
        grid=grid,
        in_specs=[x_spec, g_spec, w_spec, pl.BlockSpec((d, nf), lambda i, j: (0, 0))],
        out_specs=[o_spec, pl.BlockSpec((tm, nf), lambda i, j: (i, 0))],
        out_shape=[jax.ShapeDtypeStruct((n, nout), BF16), jax.ShapeDtypeStruct((n, nf), F32)],
        scratch_shapes=scratch, compiler_params=params, name="norm_mm_gate")(x, g2, w, wf)


def _mm_res_body(a_ref, w_ref, r_ref, o_ref):
    d = jnp.dot(a_ref[...], w_ref[...], preferred_element_type=F32)

    @pl.when(pl.program_id(2) == 0)
    def _():
        o_ref[...] = r_ref[...] + d

    @pl.when(pl.program_id(2) != 0)
    def _():
        o_ref[...] += d


def _mm_res(a, w, r, *, tm=1024, tn=1024, tk=4096):
    n, kdim = a.shape
    nout = w.shape[1]
    tm, tn, tk = min(tm, n), min(tn, nout), min(tk, kdim)
    return pl.pallas_call(
        _mm_res_body,
        grid=(n // tm, nout // tn, kdim // tk),
        in_specs=[pl.BlockSpec((tm, tk), lambda i, j, k: (i, k)),
                  pl.BlockSpec((tk, tn), lambda i, j, k: (k, j)),
                  pl.BlockSpec((tm, tn), lambda i, j, k: (i, j))],
        out_specs=pl.BlockSpec((tm, tn), lambda i, j, k: (i, j)),
        out_shape=jax.ShapeDtypeStruct((n, nout), F32),
        compiler_params=_cparams(("parallel", "parallel", "arbitrary"), 58),
        name="mm_res")(a, w, r)


def _mm2_res_body(a1_ref, a2_ref, w_ref, r_ref, o_ref):
    k1 = a1_ref.shape[1]
    o_ref[...] = (r_ref[...]
                  + jnp.dot(a1_ref[...], w_ref[:k1, :], preferred_element_type=F32)
                  + jnp.dot(a2_ref[...], w_ref[k1:, :], preferred_element_type=F32))


def _mm2_res(a1, a2, w, r, *, a2_block=0, tm=512):
    n = a1.shape[0]
    kdim, nout = w.shape
    k1 = k2 = kdim // 2
    tm = min(tm, n)
    return pl.pallas_call(
        _mm2_res_body,
        grid=(n // tm,),
        in_specs=[pl.BlockSpec((tm, k1), lambda i: (i, 0)),
                  pl.BlockSpec((tm, k2), lambda i: (i, a2_block)),
                  pl.BlockSpec((k1 + k2, nout), lambda i: (0, 0)),
                  pl.BlockSpec((tm, nout), lambda i: (i, 0))],
        out_specs=pl.BlockSpec((tm, nout), lambda i: (i, 0)),
        out_shape=jax.ShapeDtypeStruct((n, nout), F32),
        compiler_params=_cparams(("parallel",), 48),
        name="mm2_res")(a1, a2, w, r)


def _block_diag_ones(group):
    r = (lax.broadcasted_iota(jnp.int32, (2 * LANES, LANES), 0) % LANES) // group
    c = lax.broadcasted_iota(jnp.int32, (2 * LANES, LANES), 1) // group
    return (r == c).astype(BF16)


def _group_rmsnorm(x, gain, bd, group):
    xf = x.astype(F32)
    sq = xf * xf
    hi = sq.astype(BF16)
    lo = (sq - hi.astype(F32)).astype(BF16)
    gs = jnp.dot(jnp.concatenate([hi, lo], axis=1), bd, preferred_element_type=F32)
    return (xf * lax.rsqrt(gs * (1.0 / group) + EPS)) * gain


def _store_transposed(dst_ref, head, row0, x):
    dst_ref[0, head, row0:row0 + LANES, :] = x.astype(dst_ref.dtype).T


def _store_values_transposed(vt_ref, v_ref, tk):
    tm = v_ref.shape[0]
    for h in range(v_ref.shape[1] // LANES):
        vt = v_ref[:, h * LANES:(h + 1) * LANES].T
        for ch in range(tm // tk):
            vt_ref[0, ch, h * LANES:(h + 1) * LANES, :] = vt[:, ch * tk:(ch + 1) * tk]


def _hyb_prep_body(gb_ref, gc_ref, u_ref, gcp_ref, up_ref, q_ref, k_ref, v_ref, cw_ref, gq_ref,
                   gk_ref, y_ref, qt_ref, ko_ref, vt_ref, *, tm, seq, tk):
    z = gc_ref[...].astype(F32) * u_ref[...].astype(F32)
    zp = gcp_ref[...].astype(F32) * up_ref[...].astype(F32)
    first_of_seq = lax.rem(pl.program_id(0) * tm, seq) == 0
    zp = jnp.where(first_of_seq, 0.0, zp)
    zext = jnp.concatenate([zp, z], axis=0)
    z1 = pltpu.roll(zext, 1, 0)[BF16_SUBLANES:]
    z2 = pltpu.roll(zext, 2, 0)[BF16_SUBLANES:]
    cw = cw_ref[...]
    y = gb_ref[...].astype(F32) * (cw[0:1] * z2 + cw[1:2] * z1 + cw[2:3] * z)
    y_ref[...] = y.astype(BF16)

    bd = _block_diag_ones(DIFF_D)
    gq, gk = gq_ref[...], gk_ref[...]
    low_half = lax.broadcasted_iota(jnp.int32, (1, LANES), 1) < DIFF_D
    for h in range(q_ref.shape[1] // LANES):
        cols = slice(h * LANES, (h + 1) * LANES)
        _store_transposed(qt_ref, h, 0, _group_rmsnorm(q_ref[:, cols], gq, bd, DIFF_D))
        kn = _group_rmsnorm(k_ref[:, cols], gk, bd, DIFF_D)
        ko_ref[:, 2 * h * LANES:(2 * h + 1) * LANES] = jnp.where(low_half, kn, 0.0).astype(BF16)
        ko_ref[:, (2 * h + 1) * LANES:(2 * h + 2) * LANES] = jnp.where(low_half, 0.0, kn).astype(BF16)
    _store_values_transposed(vt_ref, v_ref, tk)


def _hyb_prep(proj, conv_w, gq, gk, *, batch, seq, tk, tm=512):
    n = proj.shape[0]
    c = conv_w.shape[1]
    tm = min(tm, seq)
    nt = seq // tm
    heads = c // LANES
    halo = BF16_SUBLANES
    prev = lambda col: (lambda i: (jnp.maximum(i * (tm // halo) - 1, 0), col))
    col = lambda b: pl.BlockSpec((tm, c), lambda i: (i, b))
    vec = lambda a: jnp.tile(a.astype(F32), LANES // a.shape[0]).reshape(1, LANES)
    lane_vec = pl.BlockSpec((1, LANES), lambda i: (0, 0))
    return pl.pallas_call(
        functools.partial(_hyb_prep_body, tm=tm, seq=seq, tk=tk),
        grid=(n // tm,),
        in_specs=[col(0), col(1), col(2),
                  pl.BlockSpec((halo, c), prev(1)), pl.BlockSpec((halo, c), prev(2)),
                  col(3), col(4), col(5),
                  pl.BlockSpec((CONV_WIDTH, c), lambda i: (0, 0)), lane_vec, lane_vec],
        out_specs=[pl.BlockSpec((tm, c), lambda i: (i, 0)),
                   pl.BlockSpec((1, heads, LANES, tm), lambda i: (i // nt, 0, 0, i % nt)),
                   pl.BlockSpec((tm, 2 * c), lambda i: (i, 0)),
                   pl.BlockSpec((1, tm // tk, c, tk), lambda i: (i // nt, i % nt, 0, 0))],
        out_shape=[jax.ShapeDtypeStruct((n, c), BF16),
                   jax.ShapeDtypeStruct((batch, heads, LANES, seq), BF16),
                   jax.ShapeDtypeStruct((n, 2 * c), BF16),
                   jax.ShapeDtypeStruct((batch, seq // tk, c, tk), BF16)],
        compiler_params=_cparams(("parallel",), 48),
        name="hyb_prep")(proj, proj, proj, proj, proj, proj, proj, proj, conv_w.astype(F32),
                         vec(gq), vec(gk))


def _fox_prep_body(q_ref, k_ref, v_ref, c_ref, gq_ref, gk_ref, qt_ref, ko_ref, vt_ref, *, tk):
    bd = _block_diag_ones(FOX_D)
    gq, gk = gq_ref[...], gk_ref[...]
    lane = lax.broadcasted_iota(jnp.int32, (1, LANES), 1)
    ones_q = jnp.where((lane >= 3) & (lane < 6), 1.0, 0.0)
    ones_k = jnp.where(lane < 3, 1.0, 0.0)
    c_hi, c_mid, c_lo = (t.astype(F32) for t in _split3(c_ref[...]))
    for h in range(q_ref.shape[1] // LANES):
        cols = slice(h * LANES, (h + 1) * LANES)
        _store_transposed(qt_ref, h, 0, _group_rmsnorm(q_ref[:, cols], gq, bd, FOX_D))
        ko_ref[:, 2 * h * LANES:(2 * h + 1) * LANES] = _group_rmsnorm(k_ref[:, cols], gk, bd, FOX_D).astype(BF16)
        hi, mid, lo = c_hi[:, h:h + 1], c_mid[:, h:h + 1], c_lo[:, h:h + 1]
        split_q = jnp.where(lane == 0, hi, jnp.where(lane == 1, mid, jnp.where(lane == 2, lo, 0.0)))
        split_k = jnp.where(lane == 3, hi, jnp.where(lane == 4, mid, jnp.where(lane == 5, lo, 0.0)))
        _store_transposed(qt_ref, h, LANES, split_q + ones_q)
        ko_ref[:, (2 * h + 1) * LANES:(2 * h + 2) * LANES] = (ones_k - split_k).astype(BF16)
    _store_values_transposed(vt_ref, v_ref, tk)


def _fox_prep(proj, c, gq, gk, *, batch, seq, width, tk, tm=512):
    n = proj.shape[0]
    tm = min(tm, seq)
    nt = seq // tm
    heads = width // LANES
    vec = lambda a: a.astype(F32).reshape(1, LANES)
    lane_vec = pl.BlockSpec((1, LANES), lambda i: (0, 0))
    return pl.pallas_call(
        functools.partial(_fox_prep_body, tk=tk),
        grid=(n // tm,),
        in_specs=[pl.BlockSpec((tm, width), lambda i: (i, 0)),
                  pl.BlockSpec((tm, width), lambda i: (i, 1)),
                  pl.BlockSpec((tm, width), lambda i: (i, 2)),
                  pl.BlockSpec((tm, LANES), lambda i: (i, 0)), lane_vec, lane_vec],
        out_specs=[pl.BlockSpec((1, heads, 2 * LANES, tm), lambda i: (i // nt, 0, 0, i % nt)),
                   pl.BlockSpec((tm, 2 * width), lambda i: (i, 0)),
                   pl.BlockSpec((1, tm // tk, width, tk), lambda i: (i // nt, i % nt, 0, 0))],
        out_shape=[jax.ShapeDtypeStruct((batch, heads, 2 * LANES, seq), BF16),
                   jax.ShapeDtypeStruct((n, 2 * width), BF16),
                   jax.ShapeDtypeStruct((batch, seq // tk, width, tk), BF16)],
        compiler_params=_cparams(("parallel",), 56),
        name="fox_prep")(proj, proj, proj, c, vec(gq), vec(gk))


def _gate_cumsum_body(f_ref, b_ref, c_ref, tri_ref, carry_ref, *, tc):
    @pl.when(pl.program_id(1) == 0)
    def _():
        carry_ref[...] = jnp.zeros_like(carry_ref)
        r = lax.broadcasted_iota(jnp.int32, (tc, tc), 0)
        c = lax.broadcasted_iota(jnp.int32, (tc, tc), 1)
        tri_ref[...] = (r >= c).astype(BF16)

    f = f_ref[...] + b_ref[...]
    logf = -(jnp.maximum(-f, 0.0) + jnp.log1p(jnp.exp(-jnp.abs(f))))
    hi, mid, lo = _split3(logf)
    tri = tri_ref[...]
    c = (jnp.dot(tri, hi, preferred_element_type=F32) + jnp.dot(tri, mid, preferred_element_type=F32)
         + jnp.dot(tri, lo, preferred_element_type=F32)) + carry_ref[0:1, :]
    carry_ref[...] = jnp.broadcast_to(c[tc - 1:tc, :], carry_ref.shape)
    c_ref[...] = c * LOG2E


def _gate_cumsum(f, b_f, *, batch, seq, tc=512):
    nf = f.shape[1]
    tc = min(tc, seq)
    nt = seq // tc
    b2 = jnp.zeros((1, nf), F32).at[0, :b_f.shape[0]].set(b_f.astype(F32))
    return pl.pallas_call(
        functools.partial(_gate_cumsum_body, tc=tc),
        grid=(batch, nt),
        in_specs=[pl.BlockSpec((tc, nf), lambda b, j: (b * nt + j, 0)),
                  pl.BlockSpec((1, nf), lambda b, j: (0, 0))],
        out_specs=pl.BlockSpec((tc, nf), lambda b, j: (b * nt + j, 0)),
        out_shape=jax.ShapeDtypeStruct(f.shape, F32),
        scratch_shapes=[pltpu.VMEM((tc, tc), BF16), pltpu.VMEM((8, nf), F32)],
        compiler_params=_cparams(("arbitrary", "arbitrary"), 32),
        name="gate_cumsum")(f, b2)


def _flash_core(qt_ref, k_ref, vt_ref, s_ref, p_ref, a_ref, m_ref, l_ref, acc_ref, x_ref,
                *, tk, nm, dk, shared_qv):
    qi = pl.program_id(2)
    tq = 2 * tk
    dv = acc_ref.shape[1]
    cw = min(SCORE_COLS, tk)
    all_tiles = [slice(a, a + cw) for a in range(0, tq, cw)]
    lower_tiles = [t for t in all_tiles if t.start >= tk]
    key_minus_query = (lax.broadcasted_iota(jnp.int32, (ROW_BLOCK, cw), 0)
                       - lax.broadcasted_iota(jnp.int32, (ROW_BLOCK, cw), 1))

    def visible_rows(tile, key_col):
        return tk if key_col is None else min(tk, tile.start - key_col + cw)

    def scores(c, slot, tiles=all_tiles, key_col=None):
        start = pl.multiple_of(c * tk, tk)

        def map_scores(rows, cols):
            ks = [k_ref[0, pl.ds(start, rows), m * dk:(m + 1) * dk] for m in range(nm)]
            if shared_qv:
                s = jnp.dot(jnp.concatenate(ks, axis=0), qt_ref[0, 0, :, cols],
                            preferred_element_type=F32)
                return [s[m * rows:(m + 1) * rows] for m in range(nm)]
            return [jnp.dot(ks[m], qt_ref[0, m, :, cols], preferred_element_type=F32)
                    for m in range(nm)]

        if key_col is None:
            span = slice(tiles[0].start, tiles[-1].stop)
            for m, s_all in enumerate(map_scores(tk, span)):
                for t in tiles:
                    s = s_all[:, t.start - span.start:t.stop - span.start]
                    s_ref[slot * nm + m, :, t] = s
                    mx = s[0:8]
                    for r in range(8, tk, 8):
                        mx = jnp.maximum(mx, s[r:r + 8])
                    x_ref[slot * nm + m, :, t] = jnp.max(mx, axis=0, keepdims=True)
        else:
            for t in tiles:
                rows = visible_rows(t, key_col)
                for m, s in enumerate(map_scores(rows, t)):
                    s_ref[slot * nm + m, 0:rows, t] = s

    def softmax(slot, tiles=all_tiles, key_col=None):
        for m in range(nm):
            idx = slot * nm + m
            for t in tiles:
                rows = visible_rows(t, key_col)
                rel = None if key_col is None else t.start - key_col

                def block(r):
                    s = s_ref[idx, r:r + ROW_BLOCK, t]
                    if rel is not None and r + ROW_BLOCK - 1 > rel:
                        s = jnp.where(key_minus_query <= rel - r, s, -jnp.inf)
                    return s

                if rel is None or rel >= tk:
                    smax = x_ref[idx, :, t]
                else:
                    mx = block(0)
                    for r in range(ROW_BLOCK, rows, ROW_BLOCK):
                        mx = jnp.maximum(mx, block(r))
                    smax = jnp.max(mx, axis=0, keepdims=True)
                m_old = m_ref[m, :, t]
                m_new = jnp.maximum(m_old, smax)
                alpha = jnp.exp2(m_old - m_new)
                psum = None
                for r in range(0, rows, ROW_BLOCK):
                    p = jnp.exp2(block(r) - m_new)
                    part = p[0:8] + p[8:16]
                    psum = part if psum is None else psum + part
                    p_ref[idx, r:r + ROW_BLOCK, t] = p.astype(BF16)
                m_ref[m, :, t] = m_new
                l_ref[m, :, t] = alpha * l_ref[m, :, t] + jnp.sum(psum, axis=0, keepdims=True)
                a_ref[idx, :, t] = alpha

    def weighted_values(c, slot, tiles=all_tiles, key_col=None):
        for m in range(nm):
            idx = slot * nm + m
            for t in tiles:
                rows = visible_rows(t, key_col)
                feat = slice(0, dv) if shared_qv else slice(m * dv, (m + 1) * dv)
                acc_ref[m, :, t] = (a_ref[idx, :, t] * acc_ref[m, :, t]
                                    + jnp.dot(vt_ref[0, c, feat, 0:rows], p_ref[idx, 0:rows, t],
                                              preferred_element_type=F32))

    m_ref[...] = jnp.full(m_ref.shape, -jnp.inf, F32)
    l_ref[...] = jnp.zeros(l_ref.shape, F32)
    acc_ref[...] = jnp.zeros(acc_ref.shape, F32)
    for m in range(nm):
        p_ref[nm + m] = jnp.zeros(p_ref.shape[1:], BF16)
        a_ref[nm + m] = jnp.ones(a_ref.shape[1:], F32)

    n_vis = 2 * qi
    scores(0, 0)

    def pair(u, carry):
        c = 2 * u
        scores(c + 1, 1)
        softmax(0)
        weighted_values(jnp.maximum(c - 1, 0), 1)
        scores(c + 2, 0)
        softmax(1)
        weighted_values(c, 0)
        return carry

    lax.fori_loop(0, qi, pair, 0)
    scores(n_vis + 1, 1, lower_tiles, key_col=tk)
    softmax(0, key_col=0)
    weighted_values(jnp.maximum(n_vis - 1, 0), 1)
    softmax(1, lower_tiles, key_col=tk)
    weighted_values(n_vis, 0, key_col=0)
    weighted_values(n_vis + 1, 1, lower_tiles, key_col=tk)


def _flash_scratch(tq, tk, nm, dv):
    return [pltpu.VMEM((2 * nm, tk, tq), F32),
            pltpu.VMEM((2 * nm, tk, tq), BF16),
            pltpu.VMEM((2 * nm, 1, tq), F32),
            pltpu.VMEM((nm, 1, tq), F32),
            pltpu.VMEM((nm, 1, tq), F32),
            pltpu.VMEM((nm, dv, tq), F32),
            pltpu.VMEM((2 * nm, 1, tq), F32)]


def _fox_attn_body(qt_ref, k_ref, vt_ref, o_ref, *scratch, tk, heads):
    _flash_core(qt_ref, k_ref, vt_ref, *scratch, tk=tk, nm=heads, dk=k_ref.shape[2] // heads,
                shared_qv=False)
    l_ref, acc_ref = scratch[4], scratch[5]
    for m in range(heads):
        o_ref[0, :, m * FOX_D:(m + 1) * FOX_D] = (acc_ref[m] / l_ref[m]).T.astype(o_ref.dtype)


def _fox_attn(qt, ka, vt, *, tk, heads_per_step=2):
    b, heads, dq, t = qt.shape
    tq = 2 * tk
    hs = heads_per_step
    return pl.pallas_call(
        functools.partial(_fox_attn_body, tk=tk, heads=hs),
        grid=(b, heads // hs, t // tq),
        in_specs=[pl.BlockSpec((1, hs, dq, tq), lambda bi, h, i: (bi, h, 0, i)),
                  pl.BlockSpec((1, t, hs * dq), lambda bi, h, i: (bi, 0, h)),
                  pl.BlockSpec((1, t // tk, hs * FOX_D, tk), lambda bi, h, i: (bi, 0, h, 0))],
        out_specs=pl.BlockSpec((1, tq, hs * FOX_D), lambda bi, h, i: (bi, i, h)),
        out_shape=jax.ShapeDtypeStruct((b, t, heads * FOX_D), BF16),
        scratch_shapes=_flash_scratch(tq, tk, hs, FOX_D),
        compiler_params=_cparams(("parallel", "parallel", "arbitrary"), 56),
        name="fox_attn")(qt, ka, vt)


def _diff_attn_body(lq1_ref, lk1_ref, lq2_ref, lk2_ref, sg_ref, qt_ref, k_ref, vt_ref, o_ref,
                    *scratch, tk, lam_init):
    _flash_core(qt_ref, k_ref, vt_ref, *scratch, tk=tk, nm=2, dk=qt_ref.shape[2], shared_qv=True)
    l_ref, acc_ref = scratch[4], scratch[5]
    lam = (jnp.exp(jnp.sum(lq1_ref[...] * lk1_ref[...], keepdims=True))
           - jnp.exp(jnp.sum(lq2_ref[...] * lk2_ref[...], keepdims=True)) + lam_init)
    o = acc_ref[0] / l_ref[0] - lam * (acc_ref[1] / l_ref[1])
    ms = jnp.mean(o * o, axis=0, keepdims=True)
    o = ((o * lax.rsqrt(ms + EPS)) * sg_ref[...]) * (1.0 - lam_init)
    o_ref[0] = o.T.astype(o_ref.dtype)


def _diff_attn(qt, kn, vt, lq1, lk1, lq2, lk2, subln_g, *, lam_init, tk):
    b, heads, dq, t = qt.shape
    tq = 2 * tk
    dv = 2 * DIFF_D
    vec = lambda a: a.reshape(1, -1).astype(F32)
    small = lambda w: pl.BlockSpec((1, w), lambda bi, h, i: (0, 0))
    return pl.pallas_call(
        functools.partial(_diff_attn_body, tk=tk, lam_init=lam_init),
        grid=(b, heads, t // tq),
        in_specs=[small(DIFF_D), small(DIFF_D), small(DIFF_D), small(DIFF_D),
                  pl.BlockSpec((dv, 1), lambda bi, h, i: (0, 0)),
                  pl.BlockSpec((1, 1, dq, tq), lambda bi, h, i: (bi, h, 0, i)),
                  pl.BlockSpec((1, t, 2 * dq), lambda bi, h, i: (bi, 0, h)),
                  pl.BlockSpec((1, t // tk, dv, tk), lambda bi, h, i: (bi, 0, h, 0))],
        out_specs=pl.BlockSpec((1, tq, dv), lambda bi, h, i: (bi, i, h)),
        out_shape=jax.ShapeDtypeStruct((b, t, heads * dv), BF16),
        scratch_shapes=_flash_scratch(tq, tk, 2, dv),
        compiler_params=_cparams(("parallel", "parallel", "arbitrary"), 48),
        name="diff_attn")(vec(lq1), vec(lk1), vec(lq2), vec(lk2),
                          subln_g.reshape(dv, 1).astype(F32), qt, kn, vt)


def kernel(x, norm1_g, norm2_g, hyb_w_in, hyb_conv_w, hyb_dq_g, hyb_dk_g, hyb_lq1, hyb_lk1,
           hyb_lq2, hyb_lk2, hyb_subln_g, hyb_w_out, fox_w_in, fox_b_f, fox_q_g, fox_k_g,
           fox_w_out, mlp_w1, mlp_w2):
    b, t, d = x.shape
    n = b * t
    depth = norm1_g.shape[0]
    diff_tk = fox_tk = min(ATTN_KEY_CHUNK, t // 2)
    xf = x.reshape(n, d)
    for l in range(depth):
        j = l // 2
        if l % 2 == 0:
            proj = _norm_mm(xf, norm1_g[l], _to_bf16(hyb_w_in, j))
            gq = hyb_dq_g[j].astype(F32) * (DIFF_D ** -0.5 * LOG2E)
            y_conv, qt, kn, vt = _hyb_prep(proj, hyb_conv_w[j], gq, hyb_dk_g[j], batch=b, seq=t,
                                           tk=diff_tk)
            att = _diff_attn(qt, kn.reshape(b, t, -1), vt, hyb_lq1[j], hyb_lk1[j], hyb_lq2[j],
                             hyb_lk2[j], hyb_subln_g[j], lam_init=0.8 - 0.6 * math.exp(-0.3 * l),
                             tk=diff_tk)
            xf = _mm2_res(y_conv, att.reshape(n, -1), _to_bf16(hyb_w_out, j), xf)
        else:
            w_in = _to_bf16(fox_w_in, j)
            wf = jnp.zeros((d, LANES), BF16).at[:, :FOX_HEADS].set(w_in[:, 3 * d:])
            proj, f = _norm_mm(xf, norm1_g[l], w_in, nout=3 * d, wf=wf)
            c = _gate_cumsum(f, fox_b_f[j], batch=b, seq=t)
            gq = fox_q_g[j].astype(F32) * (FOX_D ** -0.5 * LOG2E)
            qt, ka, vt = _fox_prep(proj, c, gq, fox_k_g[j], batch=b, seq=t, width=d, tk=fox_tk)
            att = _fox_attn(qt, ka.reshape(b, t, -1), vt, tk=fox_tk).reshape(n, -1)
            xf = _mm2_res(att, att, _to_bf16(fox_w_out, j), xf, a2_block=1)
        hid = _norm_mm(xf, norm2_g[l], _to_bf16(mlp_w1, l), sqrelu=True)
        xf = _mm_res(hid, _to_bf16(mlp_w2, l), xf)
    return xf.reshape(b, t, d)
```

```python
import functools
import math

import jax
import jax.numpy as jnp
from jax import lax
from jax.experimental import pallas as pl
from jax.experimental.pallas import tpu as pltpu

F32 = jnp.float32
BF16 = jnp.bfloat16
EPS = 1e-6
LOG2E = math.log2(math.e)
LANES = 128
BF16_SUBLANES = 16
MIB = 2 ** 20

CONV_WIDTH = 3
DIFF_HEADS = 8
DIFF_D = 64
FOX_HEADS = 16
FOX_D = 128

ATTN_KEY_CHUNK = 512
SCORE_COLS = 256
ROW_BLOCK = BF16_SUBLANES


def _cparams(semantics, vmem_mib):
    return pltpu.CompilerParams(dimension_semantics=semantics, vmem_limit_bytes=vmem_mib * MIB)


def _split3(x):
    hi = x.astype(BF16)
    r1 = x - hi.astype(F32)
    mid = r1.astype(BF16)
    lo = (r1 - mid.astype(F32)).astype(BF16)
    return hi, mid, lo


CAST_BLOCK_ELEMS = 2 ** 21


def _cast_body(w_ref, o_ref):
    o_ref[...] = w_ref[0].astype(o_ref.dtype)


def _to_bf16(w_stack, layer):
    _, rows, cols = w_stack.shape
    tr = min(rows, max(BF16_SUBLANES, CAST_BLOCK_ELEMS // cols // BF16_SUBLANES * BF16_SUBLANES))
    while rows % tr:
        tr -= BF16_SUBLANES
    return pl.pallas_call(
        _cast_body,
        grid=(rows // tr,),
        in_specs=[pl.BlockSpec((1, tr, cols), lambda i: (layer, i, 0))],
        out_specs=pl.BlockSpec((tr, cols), lambda i: (i, 0)),
        out_shape=jax.ShapeDtypeStruct((rows, cols), BF16),
        compiler_params=_cparams(("parallel",), 40),
        name="to_bf16")(w_stack)


def _normalize_rows(x_ref, g_ref, h_ref):
    x = x_ref[...]
    ms = jnp.mean(x * x, axis=-1, keepdims=True)
    h_ref[...] = ((x * lax.rsqrt(ms + EPS)) * g_ref[...]).astype(BF16)


def _norm_mm_body(x_ref, g_ref, w_ref, o_ref, h_ref, *, sqrelu):
    def project():
        y = jnp.dot(h_ref[...], w_ref[...], preferred_element_type=F32)
        if sqrelu:
            y = jnp.square(jnp.maximum(y, 0.0))
        o_ref[...] = y.astype(o_ref.dtype)

    @pl.when(pl.program_id(1) == 0)
    def _():
        _normalize_rows(x_ref, g_ref, h_ref)
        project()

    @pl.when(pl.program_id(1) != 0)
    def _():
        project()


def _norm_mm_gate_body(x_ref, g_ref, w_ref, wf_ref, o_ref, f_ref, h_ref):
    def project():
        o_ref[...] = jnp.dot(h_ref[...], w_ref[...], preferred_element_type=F32).astype(o_ref.dtype)

    @pl.when(pl.program_id(1) == 0)
    def _():
        _normalize_rows(x_ref, g_ref, h_ref)
        f_ref[...] = jnp.dot(h_ref[...], wf_ref[...], preferred_element_type=F32)
        project()

    @pl.when(pl.program_id(1) != 0)
    def _():
        project()


def _norm_mm(x, g, w, *, nout=None, sqrelu=False, wf=None, tm=1024, tn=2048):
    n, d = x.shape
    nout = w.shape[1] if nout is None else nout
    tm, tn = min(tm, n), min(tn, nout)
    grid = (n // tm, nout // tn)
    x_spec = pl.BlockSpec((tm, d), lambda i, j: (i, 0))
    g_spec = pl.BlockSpec((1, d), lambda i, j: (0, 0))
    w_spec = pl.BlockSpec((d, tn), lambda i, j: (0, j))
    o_spec = pl.BlockSpec((tm, tn), lambda i, j: (i, j))
    scratch = [pltpu.VMEM((tm, d), BF16)]
    params = _cparams(("parallel", "arbitrary"), 58)
    g2 = g.reshape(1, d).astype(F32)
    if wf is None:
        return pl.pallas_call(
            functools.partial(_norm_mm_body, sqrelu=sqrelu),
            grid=grid, in_specs=[x_spec, g_spec, w_spec], out_specs=o_spec,
            out_shape=jax.ShapeDtypeStruct((n, nout), BF16),
            scratch_shapes=scratch, compiler_params=params, name="norm_mm")(x, g2, w)
    nf = wf.shape[1]
    return pl.pallas_call(
        _norm_mm_gate_body,
        grid=grid,
        in_specs=[x_spec, g_spec, w_spec, pl.BlockSpec((d, nf), lambda i, j: (0, 0))],
        out_specs=[o_spec, pl.BlockSpec((tm, nf), lambda i, j: (i, 0))],
        out_shape=[jax.ShapeDtypeStruct((n, nout), BF16), jax.ShapeDtypeStruct((n, nf), F32)],
        scratch_shapes=scratch, compiler_params=params, name="norm_mm_gate")(x, g2, w, wf)


def _mm_res_body(a_ref, w_ref, r_ref, o_ref):
    @pl.when(pl.program_id(2) == 0)
    def _():
        o_ref[...] = r_ref[...] + jnp.dot(a_ref[...], w_ref[...], preferred_element_type=F32)

    @pl.when(pl.program_id(2) != 0)
    def _():
        o_ref[...] += jnp.dot(a_ref[...], w_ref[...], preferred_element_type=F32)


def _mm_res(a, w, r, *, tm=1024, tn=1024, tk=4096):
    n, kdim = a.shape
    nout = w.shape[1]
    tm, tn, tk = min(tm, n), min(tn, nout), min(tk, kdim)
    return pl.pallas_call(
        _mm_res_body,
        grid=(n // tm, nout // tn, kdim // tk),
        in_specs=[pl.BlockSpec((tm, tk), lambda i, j, k: (i, k)),
                  pl.BlockSpec((tk, tn), lambda i, j, k: (k, j)),
                  pl.BlockSpec((tm, tn), lambda i, j, k: (i, j))],
        out_specs=pl.BlockSpec((tm, tn), lambda i, j, k: (i, j)),
        out_shape=jax.ShapeDtypeStruct((n, nout), F32),
        compiler_params=_cparams(("parallel", "parallel", "arbitrary"), 58),
        name="mm_res")(a, w, r)


def _mm2_res_body(a1_ref, a2_ref, w_ref, r_ref, o_ref):
    k1 = a1_ref.shape[1]
    o_ref[...] = (r_ref[...]
                  + jnp.dot(a1_ref[...], w_ref[:k1, :], preferred_element_type=F32)
                  + jnp.dot(a2_ref[...], w_ref[k1:, :], preferred_element_type=F32))


def _mm2_res(a1, a2, w, r, *, a2_block=0, tm=512):
    n = a1.shape[0]
    kdim, nout = w.shape
    k1 = k2 = kdim // 2
    tm = min(tm, n)
    return pl.pallas_call(
        _mm2_res_body,
        grid=(n // tm,),
        in_specs=[pl.BlockSpec((tm, k1), lambda i: (i, 0)),
                  pl.BlockSpec((tm, k2), lambda i: (i, a2_block)),
                  pl.BlockSpec((k1 + k2, nout), lambda i: (0, 0)),
                  pl.BlockSpec((tm, nout), lambda i: (i, 0))],
        out_specs=pl.BlockSpec((tm, nout), lambda i: (i, 0)),
        out_shape=jax.ShapeDtypeStruct((n, nout), F32),
        compiler_params=_cparams(("parallel",), 48),
        name="mm2_res")(a1, a2, w, r)


def _block_diag_ones(group):
    r = (lax.broadcasted_iota(jnp.int32, (2 * LANES, LANES), 0) % LANES) // group
    c = lax.broadcasted_iota(jnp.int32, (2 * LANES, LANES), 1) // group
    return (r == c).astype(BF16)


def _group_rmsnorm(x, gain, bd, group):
    xf = x.astype(F32)
    sq = xf * xf
    hi = sq.astype(BF16)
    lo = (sq - hi.astype(F32)).astype(BF16)
    gs = jnp.dot(jnp.concatenate([hi, lo], axis=1), bd, preferred_element_type=F32)
    return (xf * lax.rsqrt(gs * (1.0 / group) + EPS)) * gain


def _store_transposed(dst_ref, head, row0, x):
    dst_ref[0, head, row0:row0 + LANES, :] = x.astype(dst_ref.dtype).T


def _store_values_transposed(vt_ref, v_ref, tk):
    tm = v_ref.shape[0]
    for h in range(v_ref.shape[1] // LANES):
        vt = v_ref[:, h * LANES:(h + 1) * LANES].T
        for ch in range(tm // tk):
            vt_ref[0, ch, h * LANES:(h + 1) * LANES, :] = vt[:, ch * tk:(ch + 1) * tk]


def _hyb_prep_body(gb_ref, gc_ref, u_ref, gcp_ref, up_ref, q_ref, k_ref, v_ref, cw_ref, gq_ref,
                   gk_ref, y_ref, qt_ref, ko_ref, vt_ref, *, tm, seq, tk):
    z = gc_ref[...].astype(F32) * u_ref[...].astype(F32)
    zp = gcp_ref[...].astype(F32) * up_ref[...].astype(F32)
    first_of_seq = lax.rem(pl.program_id(0) * tm, seq) == 0
    zp = jnp.where(first_of_seq, 0.0, zp)
    zext = jnp.concatenate([zp, z], axis=0)
    z1 = pltpu.roll(zext, 1, 0)[BF16_SUBLANES:]
    z2 = pltpu.roll(zext, 2, 0)[BF16_SUBLANES:]
    cw = cw_ref[...]
    y = gb_ref[...].astype(F32) * (cw[0:1] * z2 + cw[1:2] * z1 + cw[2:3] * z)
    y_ref[...] = y.astype(BF16)

    bd = _block_diag_ones(DIFF_D)
    gq, gk = gq_ref[...], gk_ref[...]
    low_half = lax.broadcasted_iota(jnp.int32, (1, LANES), 1) < DIFF_D
    for h in range(q_ref.shape[1] // LANES):
        cols = slice(h * LANES, (h + 1) * LANES)
        _store_transposed(qt_ref, h, 0, _group_rmsnorm(q_ref[:, cols], gq, bd, DIFF_D))
        kn = _group_rmsnorm(k_ref[:, cols], gk, bd, DIFF_D)
        ko_ref[:, 2 * h * LANES:(2 * h + 1) * LANES] = jnp.where(low_half, kn, 0.0).astype(BF16)
        ko_ref[:, (2 * h + 1) * LANES:(2 * h + 2) * LANES] = jnp.where(low_half, 0.0, kn).astype(BF16)
    _store_values_transposed(vt_ref, v_ref, tk)


def _hyb_prep(proj, conv_w, gq, gk, *, batch, seq, tk, tm=512):
    n = proj.shape[0]
    c = conv_w.shape[1]
    tm = min(tm, seq)
    nt = seq // tm
    heads = c // LANES
    halo = BF16_SUBLANES
    prev = lambda col: (lambda i: (jnp.maximum(i * (tm // halo) - 1, 0), col))
    col = lambda b: pl.BlockSpec((tm, c), lambda i: (i, b))
    vec = lambda a: jnp.tile(a.astype(F32), LANES // a.shape[0]).reshape(1, LANES)
    lane_vec = pl.BlockSpec((1, LANES), lambda i: (0, 0))
    return pl.pallas_call(
        functools.partial(_hyb_prep_body, tm=tm, seq=seq, tk=tk),
        grid=(n // tm,),
        in_specs=[col(0), col(1), col(2),
                  pl.BlockSpec((halo, c), prev(1)), pl.BlockSpec((halo, c), prev(2)),
                  col(3), col(4), col(5),
                  pl.BlockSpec((CONV_WIDTH, c), lambda i: (0, 0)), lane_vec, lane_vec],
        out_specs=[pl.BlockSpec((tm, c), lambda i: (i, 0)),
                   pl.BlockSpec((1, heads, LANES, tm), lambda i: (i // nt, 0, 0, i % nt)),
                   pl.BlockSpec((tm, 2 * c), lambda i: (i, 0)),
                   pl.BlockSpec((1, tm // tk, c, tk), lambda i: (i // nt, i % nt, 0, 0))],
        out_shape=[jax.ShapeDtypeStruct((n, c), BF16),
                   jax.ShapeDtypeStruct((batch, heads, LANES, seq), BF16),
                   jax.ShapeDtypeStruct((n, 2 * c), BF16),
                   jax.ShapeDtypeStruct((batch, seq // tk, c, tk), BF16)],
        compiler_params=_cparams(("parallel",), 48),
        name="hyb_prep")(proj, proj, proj, proj, proj, proj, proj, proj, conv_w.astype(F32),
                         vec(gq), vec(gk))


def _fox_prep_body(q_ref, k_ref, v_ref, c_ref, gq_ref, gk_ref, qt_ref, ko_ref, vt_ref, *, tk):
    bd = _block_diag_ones(FOX_D)
    gq, gk = gq_ref[...], gk_ref[...]
    lane = lax.broadcasted_iota(jnp.int32, (1, LANES), 1)
    ones_q = jnp.where((lane >= 3) & (lane < 6), 1.0, 0.0)
    ones_k = jnp.where(lane < 3, 1.0, 0.0)
    c_hi, c_mid, c_lo = (t.astype(F32) for t in _split3(c_ref[...]))
    for h in range(q_ref.shape[1] // LANES):
        cols = slice(h * LANES, (h + 1) * LANES)
        _store_transposed(qt_ref, h, 0, _group_rmsnorm(q_ref[:, cols], gq, bd, FOX_D))
        ko_ref[:, 2 * h * LANES:(2 * h + 1) * LANES] = _group_rmsnorm(k_ref[:, cols], gk, bd, FOX_D).astype(BF16)
        hi, mid, lo = c_hi[:, h:h + 1], c_mid[:, h:h + 1], c_lo[:, h:h + 1]
        split_q = jnp.where(lane == 0, hi, jnp.where(lane == 1, mid, jnp.where(lane == 2, lo, 0.0)))
        split_k = jnp.where(lane == 3, hi, jnp.where(lane == 4, mid, jnp.where(lane == 5, lo, 0.0)))
        _store_transposed(qt_ref, h, LANES, split_q + ones_q)
        ko_ref[:, (2 * h + 1) * LANES:(2 * h + 2) * LANES] = (ones_k - split_k).astype(BF16)
    _store_values_transposed(vt_ref, v_ref, tk)


def _fox_prep(proj, c, gq, gk, *, batch, seq, width, tk, tm=512):
    n = proj.shape[0]
    tm = min(tm, seq)
    nt = seq // tm
    heads = width // LANES
    vec = lambda a: a.astype(F32).reshape(1, LANES)
    lane_vec = pl.BlockSpec((1, LANES), lambda i: (0, 0))
    return pl.pallas_call(
        functools.partial(_fox_prep_body, tk=tk),
        grid=(n // tm,),
        in_specs=[pl.BlockSpec((tm, width), lambda i: (i, 0)),
                  pl.BlockSpec((tm, width), lambda i: (i, 1)),
                  pl.BlockSpec((tm, width), lambda i: (i, 2)),
                  pl.BlockSpec((tm, LANES), lambda i: (i, 0)), lane_vec, lane_vec],
        out_specs=[pl.BlockSpec((1, heads, 2 * LANES, tm), lambda i: (i // nt, 0, 0, i % nt)),
                   pl.BlockSpec((tm, 2 * width), lambda i: (i, 0)),
                   pl.BlockSpec((1, tm // tk, width, tk), lambda i: (i // nt, i % nt, 0, 0))],
        out_shape=[jax.ShapeDtypeStruct((batch, heads, 2 * LANES, seq), BF16),
                   jax.ShapeDtypeStruct((n, 2 * width), BF16),
                   jax.ShapeDtypeStruct((batch, seq // tk, width, tk), BF16)],
        compiler_params=_cparams(("parallel",), 56),
        name="fox_prep")(proj, proj, proj, c, vec(gq), vec(gk))


def _gate_cumsum_body(f_ref, b_ref, c_ref, tri_ref, carry_ref, *, tc):
    @pl.when(pl.program_id(1) == 0)
    def _():
        carry_ref[...] = jnp.zeros_like(carry_ref)
        r = lax.broadcasted_iota(jnp.int32, (tc, tc), 0)
        c = lax.broadcasted_iota(jnp.int32, (tc, tc), 1)
        tri_ref[...] = (r >= c).astype(BF16)

    f = f_ref[...] + b_ref[...]
    logf = -(jnp.maximum(-f, 0.0) + jnp.log1p(jnp.exp(-jnp.abs(f))))
    hi, mid, lo = _split3(logf)
    tri = tri_ref[...]
    c = (jnp.dot(tri, hi, preferred_element_type=F32) + jnp.dot(tri, mid, preferred_element_type=F32)
         + jnp.dot(tri, lo, preferred_element_type=F32)) + carry_ref[0:1, :]
    carry_ref[...] = jnp.broadcast_to(c[tc - 1:tc, :], carry_ref.shape)
    c_ref[...] = c * LOG2E


def _gate_cumsum(f, b_f, *, batch, seq, tc=512):
    nf = f.shape[1]
    tc = min(tc, seq)
    nt = seq // tc
    b2 = jnp.zeros((1, nf), F32).at[0, :b_f.shape[0]].set(b_f.astype(F32))
    return pl.pallas_call(
        functools.partial(_gate_cumsum_body, tc=tc),
        grid=(batch, nt),
        in_specs=[pl.BlockSpec((tc, nf), lambda b, j: (b * nt + j, 0)),
                  pl.BlockSpec((1, nf), lambda b, j: (0, 0))],
        out_specs=pl.BlockSpec((tc, nf), lambda b, j: (b * nt + j, 0)),
        out_shape=jax.ShapeDtypeStruct(f.shape, F32),
        scratch_shapes=[pltpu.VMEM((tc, tc), BF16), pltpu.VMEM((8, nf), F32)],
        compiler_params=_cparams(("arbitrary", "arbitrary"), 32),
        name="gate_cumsum")(f, b2)


def _flash_core(qt_ref, k_ref, vt_ref, s_ref, p_ref, a_ref, m_ref, l_ref, acc_ref, x_ref,
                *, tk, nm, dk, shared_qv):
    qi = pl.program_id(2)
    tq = 2 * tk
    dv = acc_ref.shape[1]
    cw = min(SCORE_COLS, tk)
    all_tiles = [slice(a, a + cw) for a in range(0, tq, cw)]
    lower_tiles = [t for t in all_tiles if t.start >= tk]
    key_minus_query = (lax.broadcasted_iota(jnp.int32, (ROW_BLOCK, cw), 0)
                       - lax.broadcasted_iota(jnp.int32, (ROW_BLOCK, cw), 1))

    def visible_rows(tile, key_col):
        return tk if key_col is None else min(tk, tile.start - key_col + cw)

    def scores(c, slot, tiles=all_tiles, key_col=None):
        start = pl.multiple_of(c * tk, tk)

        def map_scores(rows, cols):
            ks = [k_ref[0, pl.ds(start, rows), m * dk:(m + 1) * dk] for m in range(nm)]
            if shared_qv:
                s = jnp.dot(jnp.concatenate(ks, axis=0), qt_ref[0, 0, :, cols],
                            preferred_element_type=F32)
                return [s[m * rows:(m + 1) * rows] for m in range(nm)]
            return [jnp.dot(ks[m], qt_ref[0, m, :, cols], preferred_element_type=F32)
                    for m in range(nm)]

        if key_col is None:
            span = slice(tiles[0].start, tiles[-1].stop)
            for m, s_all in enumerate(map_scores(tk, span)):
                for t in tiles:
                    s = s_all[:, t.start - span.start:t.stop - span.start]
                    s_ref[slot * nm + m, :, t] = s
                    mx = s[0:8]
                    for r in range(8, tk, 8):
                        mx = jnp.maximum(mx, s[r:r + 8])
                    x_ref[slot * nm + m, :, t] = jnp.max(mx, axis=0, keepdims=True)
        else:
            for t in tiles:
                rows = visible_rows(t, key_col)
                for m, s in enumerate(map_scores(rows, t)):
                    s_ref[slot * nm + m, 0:rows, t] = s

    def softmax(slot, tiles=all_tiles, key_col=None):
        for m in range(nm):
            idx = slot * nm + m
            for t in tiles:
                rows = visible_rows(t, key_col)
                rel = None if key_col is None else t.start - key_col

                def block(r):
                    s = s_ref[idx, r:r + ROW_BLOCK, t]
                    if rel is not None and r + ROW_BLOCK - 1 > rel:
                        s = jnp.where(key_minus_query <= rel - r, s, -jnp.inf)
                    return s

                if rel is None or rel >= tk:
                    smax = x_ref[idx, :, t]
                else:
                    mx = block(0)
                    for r in range(ROW_BLOCK, rows, ROW_BLOCK):
                        mx = jnp.maximum(mx, block(r))
                    smax = jnp.max(mx, axis=0, keepdims=True)
                m_old = m_ref[m, :, t]
                m_new = jnp.maximum(m_old, smax)
                alpha = jnp.exp2(m_old - m_new)
                psum = None
                for r in range(0, rows, ROW_BLOCK):
                    p = jnp.exp2(block(r) - m_new)
                    part = p[0:8]
                    for i in range(8, ROW_BLOCK, 8):
                        part = part + p[i:i + 8]
                    psum = part if psum is None else psum + part
                    p_ref[idx, r:r + ROW_BLOCK, t] = p.astype(BF16)
                m_ref[m, :, t] = m_new
                l_ref[m, :, t] = alpha * l_ref[m, :, t] + jnp.sum(psum, axis=0, keepdims=True)
                a_ref[idx, :, t] = alpha

    def weighted_values(c, slot, tiles=all_tiles, key_col=None):
        for m in range(nm):
            idx = slot * nm + m
            for t in tiles:
                rows = visible_rows(t, key_col)
                feat = slice(0, dv) if shared_qv else slice(m * dv, (m + 1) * dv)
                acc_ref[m, :, t] = (a_ref[idx, :, t] * acc_ref[m, :, t]
                                    + jnp.dot(vt_ref[0, c, feat, 0:rows], p_ref[idx, 0:rows, t],
                                              preferred_element_type=F32))

    m_ref[...] = jnp.full(m_ref.shape, -jnp.inf, F32)
    l_ref[...] = jnp.zeros(l_ref.shape, F32)
    acc_ref[...] = jnp.zeros(acc_ref.shape, F32)
    for m in range(nm):
        p_ref[nm + m] = jnp.zeros(p_ref.shape[1:], BF16)
        a_ref[nm + m] = jnp.ones(a_ref.shape[1:], F32)

    n_vis = 2 * qi
    scores(0, 0)

    def pair(u, carry):
        c = 2 * u
        scores(c + 1, 1)
        softmax(0)
        weighted_values(jnp.maximum(c - 1, 0), 1)
        scores(c + 2, 0)
        softmax(1)
        weighted_values(c, 0)
        return carry

    lax.fori_loop(0, qi, pair, 0)
    scores(n_vis + 1, 1, lower_tiles, key_col=tk)
    softmax(0, key_col=0)
    weighted_values(jnp.maximum(n_vis - 1, 0), 1)
    softmax(1, lower_tiles, key_col=tk)
    weighted_values(n_vis, 0, key_col=0)
    weighted_values(n_vis + 1, 1, lower_tiles, key_col=tk)


def _flash_scratch(tq, tk, nm, dv):
    return [pltpu.VMEM((2 * nm, tk, tq), F32),
            pltpu.VMEM((2 * nm, tk, tq), BF16),
            pltpu.VMEM((2 * nm, 1, tq), F32),
            pltpu.VMEM((nm, 1, tq), F32),
            pltpu.VMEM((nm, 1, tq), F32),
            pltpu.VMEM((nm, dv, tq), F32),
            pltpu.VMEM((2 * nm, 1, tq), F32)]


def _fox_attn_body(qt_ref, k_ref, vt_ref, o_ref, *scratch, tk, heads):
    _flash_core(qt_ref, k_ref, vt_ref, *scratch, tk=tk, nm=heads, dk=k_ref.shape[2] // heads,
                shared_qv=False)
    l_ref, acc_ref = scratch[4], scratch[5]
    for m in range(heads):
        o_ref[0, :, m * FOX_D:(m + 1) * FOX_D] = (acc_ref[m] / l_ref[m]).T.astype(o_ref.dtype)


def _fox_attn(qt, ka, vt, *, tk, heads_per_step=2):
    b, heads, dq, t = qt.shape
    tq = 2 * tk
    hs = heads_per_step
    return pl.pallas_call(
        functools.partial(_fox_attn_body, tk=tk, heads=hs),
        grid=(b, heads // hs, t // tq),
        in_specs=[pl.BlockSpec((1, hs, dq, tq), lambda bi, h, i: (bi, h, 0, i)),
                  pl.BlockSpec((1, t, hs * dq), lambda bi, h, i: (bi, 0, h)),
                  pl.BlockSpec((1, t // tk, hs * FOX_D, tk), lambda bi, h, i: (bi, 0, h, 0))],
        out_specs=pl.BlockSpec((1, tq, hs * FOX_D), lambda bi, h, i: (bi, i, h)),
        out_shape=jax.ShapeDtypeStruct((b, t, heads * FOX_D), BF16),
        scratch_shapes=_flash_scratch(tq, tk, hs, FOX_D),
        compiler_params=_cparams(("parallel", "parallel", "arbitrary"), 56),
        name="fox_attn")(qt, ka, vt)


def _diff_attn_body(lq1_ref, lk1_ref, lq2_ref, lk2_ref, sg_ref, qt_ref, k_ref, vt_ref, o_ref,
                    *scratch, tk, lam_init):
    _flash_core(qt_ref, k_ref, vt_ref, *scratch, tk=tk, nm=2, dk=qt_ref.shape[2], shared_qv=True)
    l_ref, acc_ref = scratch[4], scratch[5]
    lam = (jnp.exp(jnp.sum(lq1_ref[...] * lk1_ref[...], keepdims=True))
           - jnp.exp(jnp.sum(lq2_ref[...] * lk2_ref[...], keepdims=True)) + lam_init)
    o = acc_ref[0] / l_ref[0] - lam * (acc_ref[1] / l_ref[1])
    ms = jnp.mean(o * o, axis=0, keepdims=True)
    o = ((o * lax.rsqrt(ms + EPS)) * sg_ref[...]) * (1.0 - lam_init)
    o_ref[0] = o.T.astype(o_ref.dtype)


def _diff_attn(qt, kn, vt, lq1, lk1, lq2, lk2, subln_g, *, lam_init, tk):
    b, heads, dq, t = qt.shape
    tq = 2 * tk
    dv = 2 * DIFF_D
    vec = lambda a: a.reshape(1, -1).astype(F32)
    small = lambda w: pl.BlockSpec((1, w), lambda bi, h, i: (0, 0))
    return pl.pallas_call(
        functools.partial(_diff_attn_body, tk=tk, lam_init=lam_init),
        grid=(b, heads, t // tq),
        in_specs=[small(DIFF_D), small(DIFF_D), small(DIFF_D), small(DIFF_D),
                  pl.BlockSpec((dv, 1), lambda bi, h, i: (0, 0)),
                  pl.BlockSpec((1, 1, dq, tq), lambda bi, h, i: (bi, h, 0, i)),
                  pl.BlockSpec((1, t, 2 * dq), lambda bi, h, i: (bi, 0, h)),
                  pl.BlockSpec((1, t // tk, dv, tk), lambda bi, h, i: (bi, 0, h, 0))],
        out_specs=pl.BlockSpec((1, tq, dv), lambda bi, h, i: (bi, i, h)),
        out_shape=jax.ShapeDtypeStruct((b, t, heads * dv), BF16),
        scratch_shapes=_flash_scratch(tq, tk, 2, dv),
        compiler_params=_cparams(("parallel", "parallel", "arbitrary"), 48),
        name="diff_attn")(vec(lq1), vec(lk1), vec(lq2), vec(lk2),
                          subln_g.reshape(dv, 1).astype(F32), qt, kn, vt)


def kernel(x, norm1_g, norm2_g, hyb_w_in, hyb_conv_w, hyb_dq_g, hyb_dk_g, hyb_lq1, hyb_lk1,
           hyb_lq2, hyb_lk2, hyb_subln_g, hyb_w_out, fox_w_in, fox_b_f, fox_q_g, fox_k_g,
           fox_w_out, mlp_w1, mlp_w2):
    b, t, d = x.shape
    n = b * t
    depth = norm1_g.shape[0]
    diff_tk = fox_tk = min(ATTN_KEY_CHUNK, t // 2)
    xf = x.reshape(n, d)
    for l in range(depth):
        j = l // 2
        if l % 2 == 0:
            proj = _norm_mm(xf, norm1_g[l], _to_bf16(hyb_w_in, j))
            gq = hyb_dq_g[j].astype(F32) * (DIFF_D ** -0.5 * LOG2E)
            y_conv, qt, kn, vt = _hyb_prep(proj, hyb_conv_w[j], gq, hyb_dk_g[j], batch=b, seq=t,
                                           tk=diff_tk)
            att = _diff_attn(qt, kn.reshape(b, t, -1), vt, hyb_lq1[j], hyb_lk1[j], hyb_lq2[j],
                             hyb_lk2[j], hyb_subln_g[j], lam_init=0.8 - 0.6 * math.exp(-0.3 * l),
                             tk=diff_tk)
            xf = _mm2_res(y_conv, att.reshape(n, -1), _to_bf16(hyb_w_out, j), xf)
        else:
            w_in = _to_bf16(fox_w_in, j)
            wf = jnp.zeros((d, LANES), BF16).at[:, :FOX_HEADS].set(w_in[:, 3 * d:])
            proj, f = _norm_mm(xf, norm1_g[l], w_in, nout=3 * d, wf=wf)
            c = _gate_cumsum(f, fox_b_f[j], batch=b, seq=t)
            gq = fox_q_g[j].astype(F32) * (FOX_D ** -0.5 * LOG2E)
            qt, ka, vt = _fox_prep(proj, c, gq, fox_k_g[j], batch=b, seq=t, width=d, tk=fox_tk)
            att = _fox_attn(qt, ka.reshape(b, t, -1), vt, tk=fox_tk).reshape(n, -1)
            xf = _mm2_res(att, att, _to_bf16(fox_w_out, j), xf, a2_block=1)
        hid = _norm_mm(xf, norm2_g[l], _to_bf16(mlp_w1, l), sqrelu=True)
        xf = _mm_res(hid, _to_bf16(mlp_w2, l), xf)
    return xf.reshape(b, t, d)
```

```python
import functools
import math

import jax
import jax.numpy as jnp
from jax import lax
from jax.experimental import pallas as pl
from jax.experimental.pallas import tpu as pltpu

F32 = jnp.float32
BF16 = jnp.bfloat16
EPS = 1e-6
LOG2E = math.log2(math.e)
LANES = 128
BF16_SUBLANES = 16
MIB = 2 ** 20

CONV_WIDTH = 3
DIFF_HEADS = 8
DIFF_D = 64
FOX_HEADS = 16
FOX_D = 128

ATTN_KEY_CHUNK = 512
SCORE_COLS = 256
ROW_BLOCK = BF16_SUBLANES


VMEM_BYTES = 64 * MIB
VMEM_COMPILER_RESERVE = 6 * MIB


def _cparams(semantics, window_bytes, resident_bytes=0):
    limit = 2 * window_bytes + resident_bytes + VMEM_COMPILER_RESERVE
    assert limit <= VMEM_BYTES, limit
    return pltpu.CompilerParams(dimension_semantics=semantics, vmem_limit_bytes=limit)


def _padded_bytes(shape, dtype):
    item = jnp.dtype(dtype).itemsize
    sub = 8 * 4 // item
    *lead, r, c = shape
    return math.prod(lead) * (-(-r // sub) * sub) * (-(-c // LANES) * LANES) * item


def _split3(x):
    hi = x.astype(BF16)
    r1 = x - hi.astype(F32)
    mid = r1.astype(BF16)
    lo = (r1 - mid.astype(F32)).astype(BF16)
    return hi, mid, lo


CAST_BLOCK_ELEMS = 2 ** 21


def _cast_body(w_ref, o_ref):
    o_ref[...] = w_ref[0].astype(o_ref.dtype)


def _to_bf16(w_stack, layer):
    _, rows, cols = w_stack.shape
    tr = min(rows, max(BF16_SUBLANES, CAST_BLOCK_ELEMS // cols // BF16_SUBLANES * BF16_SUBLANES))
    while rows % tr:
        tr -= BF16_SUBLANES
    return pl.pallas_call(
        _cast_body,
        grid=(rows // tr,),
        in_specs=[pl.BlockSpec((1, tr, cols), lambda i: (layer, i, 0))],
        out_specs=pl.BlockSpec((tr, cols), lambda i: (i, 0)),
        out_shape=jax.ShapeDtypeStruct((rows, cols), BF16),
        compiler_params=_cparams(("parallel",), tr * cols * (4 + 2)),
        name="to_bf16")(w_stack)


def _normalize_rows(x_ref, g_ref, h_ref):
    x = x_ref[...]
    ms = jnp.mean(x * x, axis=-1, keepdims=True)
    h_ref[...] = ((x * lax.rsqrt(ms + EPS)) * g_ref[...]).astype(BF16)


def _norm_mm_body(x_ref, g_ref, w_ref, o_ref, h_ref, *, sqrelu):
    def project():
        y = jnp.dot(h_ref[...], w_ref[...], preferred_element_type=F32)
        if sqrelu:
            y = jnp.square(jnp.maximum(y, 0.0))
        o_ref[...] = y.astype(o_ref.dtype)

    @pl.when(pl.program_id(1) == 0)
    def _():
        _normalize_rows(x_ref, g_ref, h_ref)
        project()

    @pl.when(pl.program_id(1) != 0)
    def _():
        project()


def _norm_mm_gate_body(x_ref, g_ref, wt_ref, wft_ref, o_ref, f_ref, h_ref):
    contract_last = (((1,), (1,)), ((), ()))

    def project():
        y = lax.dot_general(h_ref[...], wt_ref[...], contract_last, preferred_element_type=F32)
        o_ref[...] = y.astype(o_ref.dtype)

    @pl.when(pl.program_id(1) == 0)
    def _():
        _normalize_rows(x_ref, g_ref, h_ref)
        f_ref[...] = lax.dot_general(h_ref[...], wft_ref[...], contract_last,
                                     preferred_element_type=F32)
        project()

    @pl.when(pl.program_id(1) != 0)
    def _():
        project()


def _norm_mm(x, g, w, *, nout=None, sqrelu=False, wf_t=None, tm=1024, tn=2048):
    n, d = x.shape
    nout = w.shape[1] if nout is None else nout
    tm, tn = min(tm, n), min(tn, nout)
    grid = (n // tm, nout // tn)
    x_spec = pl.BlockSpec((tm, d), lambda i, j: (i, 0))
    g_spec = pl.BlockSpec((1, d), lambda i, j: (0, 0))
    w_spec = pl.BlockSpec((d, tn), lambda i, j: (0, j))
    o_spec = pl.BlockSpec((tm, tn), lambda i, j: (i, j))
    scratch = [pltpu.VMEM((tm, d), BF16)]
    nf = 0 if wf_t is None else wf_t.shape[0]
    params = _cparams(("parallel", "arbitrary"),
                      tm * d * 4 + d * tn * 2 + tm * tn * 2 + d * nf * 2 + tm * nf * 4, tm * d * 2)
    g2 = g.reshape(1, d).astype(F32)
    if wf_t is None:
        return pl.pallas_call(
            functools.partial(_norm_mm_body, sqrelu=sqrelu),
            grid=grid, in_specs=[x_spec, g_spec, w_spec], out_specs=o_spec,
            out_shape=jax.ShapeDtypeStruct((n, nout), BF16),
            scratch_shapes=scratch, compiler_params=params, name="norm_mm")(x, g2, w)
    return pl.pallas_call(
        _norm_mm_gate_body,
        grid=grid,
        in_specs=[x_spec, g_spec, pl.BlockSpec((tn, d), lambda i, j: (j, 0)),
                  pl.BlockSpec((nf, d), lambda i, j: (0, 0))],
        out_specs=[o_spec, pl.BlockSpec((tm, nf), lambda i, j: (i, 0))],
        out_shape=[jax.ShapeDtypeStruct((n, nout), BF16), jax.ShapeDtypeStruct((n, nf), F32)],
        scratch_shapes=scratch, compiler_params=params, name="norm_mm_gate")(x, g2, w, wf_t)


def _mm_res_body(a_ref, w_ref, r_ref, o_ref):
    @pl.when(pl.program_id(2) == 0)
    def _():
        o_ref[...] = r_ref[...] + jnp.dot(a_ref[...], w_ref[...], preferred_element_type=F32)

    @pl.when(pl.program_id(2) != 0)
    def _():
        o_ref[...] += jnp.dot(a_ref[...], w_ref[...], preferred_element_type=F32)


def _mm_res(a, w, r, *, tm=1024, tn=1024, tk=4096):
    n, kdim = a.shape
    nout = w.shape[1]
    tm, tn, tk = min(tm, n), min(tn, nout), min(tk, kdim)
    return pl.pallas_call(
        _mm_res_body,
        grid=(n // tm, nout // tn, kdim // tk),
        in_specs=[pl.BlockSpec((tm, tk), lambda i, j, k: (i, k)),
                  pl.BlockSpec((tk, tn), lambda i, j, k: (k, j)),
                  pl.BlockSpec((tm, tn), lambda i, j, k: (i, j))],
        out_specs=pl.BlockSpec((tm, tn), lambda i, j, k: (i, j)),
        out_shape=jax.ShapeDtypeStruct((n, nout), F32),
        compiler_params=_cparams(("parallel", "parallel", "arbitrary"),
                                 tm * tk * 2 + tk * tn * 2 + 2 * tm * tn * 4, tm * tn * 4),
        name="mm_res")(a, w, r)


def _mm2_res_body(a1_ref, a2_ref, w_ref, r_ref, o_ref):
    k1 = a1_ref.shape[1]
    o_ref[...] = (r_ref[...]
                  + jnp.dot(a1_ref[...], w_ref[:k1, :], preferred_element_type=F32)
                  + jnp.dot(a2_ref[...], w_ref[k1:, :], preferred_element_type=F32))


def _mm2_res(a1, a2, w, r, *, a2_block=0, tm=512):
    n = a1.shape[0]
    kdim, nout = w.shape
    k1 = k2 = kdim // 2
    tm = min(tm, n)
    return pl.pallas_call(
        _mm2_res_body,
        grid=(n // tm,),
        in_specs=[pl.BlockSpec((tm, k1), lambda i: (i, 0)),
                  pl.BlockSpec((tm, k2), lambda i: (i, a2_block)),
                  pl.BlockSpec((k1 + k2, nout), lambda i: (0, 0)),
                  pl.BlockSpec((tm, nout), lambda i: (i, 0))],
        out_specs=pl.BlockSpec((tm, nout), lambda i: (i, 0)),
        out_shape=jax.ShapeDtypeStruct((n, nout), F32),
        compiler_params=_cparams(("parallel",), tm * kdim * 2 + kdim * nout * 2 + 2 * tm * nout * 4,
                                 tm * nout * 4),
        name="mm2_res")(a1, a2, w, r)


def _block_diag_ones(group):
    r = (lax.broadcasted_iota(jnp.int32, (2 * LANES, LANES), 0) % LANES) // group
    c = lax.broadcasted_iota(jnp.int32, (2 * LANES, LANES), 1) // group
    return (r == c).astype(BF16)


def _group_rmsnorm(x, gain, bd, group):
    xf = x.astype(F32)
    sq = xf * xf
    hi = sq.astype(BF16)
    lo = (sq - hi.astype(F32)).astype(BF16)
    gs = jnp.dot(jnp.concatenate([hi, lo], axis=1), bd, preferred_element_type=F32)
    return (xf * lax.rsqrt(gs * (1.0 / group) + EPS)) * gain


def _store_transposed(dst_ref, head, row0, x):
    dst_ref[0, head, row0:row0 + LANES, :] = x.astype(dst_ref.dtype).T


def _store_values_transposed(vt_ref, v_ref, tk):
    tm = v_ref.shape[0]
    for h in range(v_ref.shape[1] // LANES):
        vt = v_ref[:, h * LANES:(h + 1) * LANES].T
        for ch in range(tm // tk):
            vt_ref[0, ch, h * LANES:(h + 1) * LANES, :] = vt[:, ch * tk:(ch + 1) * tk]


def _hyb_prep_body(gb_ref, gc_ref, u_ref, gcp_ref, up_ref, q_ref, k_ref, v_ref, cw_ref, gq_ref,
                   gk_ref, y_ref, qt_ref, ko_ref, vt_ref, *, tm, seq, tk):
    z = gc_ref[...].astype(F32) * u_ref[...].astype(F32)
    zp = gcp_ref[...].astype(F32) * up_ref[...].astype(F32)
    first_of_seq = lax.rem(pl.program_id(0) * tm, seq) == 0
    zp = jnp.where(first_of_seq, 0.0, zp)
    zext = jnp.concatenate([zp, z], axis=0)
    z1 = pltpu.roll(zext, 1, 0)[BF16_SUBLANES:]
    z2 = pltpu.roll(zext, 2, 0)[BF16_SUBLANES:]
    cw = cw_ref[...]
    y = gb_ref[...].astype(F32) * (cw[0:1] * z2 + cw[1:2] * z1 + cw[2:3] * z)
    y_ref[...] = y.astype(BF16)

    bd = _block_diag_ones(DIFF_D)
    gq, gk = gq_ref[...], gk_ref[...]
    low_half = lax.broadcasted_iota(jnp.int32, (1, LANES), 1) < DIFF_D
    for h in range(q_ref.shape[1] // LANES):
        cols = slice(h * LANES, (h + 1) * LANES)
        _store_transposed(qt_ref, h, 0, _group_rmsnorm(q_ref[:, cols], gq, bd, DIFF_D))
        kn = _group_rmsnorm(k_ref[:, cols], gk, bd, DIFF_D)
        ko_ref[:, 2 * h * LANES:(2 * h + 1) * LANES] = jnp.where(low_half, kn, 0.0).astype(BF16)
        ko_ref[:, (2 * h + 1) * LANES:(2 * h + 2) * LANES] = jnp.where(low_half, 0.0, kn).astype(BF16)
    _store_values_transposed(vt_ref, v_ref, tk)


def _hyb_prep(proj, conv_w, gq, gk, *, batch, seq, tk, tm=512):
    n = proj.shape[0]
    c = conv_w.shape[1]
    tm = min(tm, seq)
    nt = seq // tm
    heads = c // LANES
    halo = BF16_SUBLANES
    prev = lambda col: (lambda i: (jnp.maximum(i * (tm // halo) - 1, 0), col))
    col = lambda b: pl.BlockSpec((tm, c), lambda i: (i, b))
    vec = lambda a: jnp.tile(a.astype(F32), LANES // a.shape[0]).reshape(1, LANES)
    lane_vec = pl.BlockSpec((1, LANES), lambda i: (0, 0))
    return pl.pallas_call(
        functools.partial(_hyb_prep_body, tm=tm, seq=seq, tk=tk),
        grid=(n // tm,),
        in_specs=[col(0), col(1), col(2),
                  pl.BlockSpec((halo, c), prev(1)), pl.BlockSpec((halo, c), prev(2)),
                  col(3), col(4), col(5),
                  pl.BlockSpec((CONV_WIDTH, c), lambda i: (0, 0)), lane_vec, lane_vec],
        out_specs=[pl.BlockSpec((tm, c), lambda i: (i, 0)),
                   pl.BlockSpec((1, heads, LANES, tm), lambda i: (i // nt, 0, 0, i % nt)),
                   pl.BlockSpec((tm, 2 * c), lambda i: (i, 0)),
                   pl.BlockSpec((1, tm // tk, c, tk), lambda i: (i // nt, i % nt, 0, 0))],
        out_shape=[jax.ShapeDtypeStruct((n, c), BF16),
                   jax.ShapeDtypeStruct((batch, heads, LANES, seq), BF16),
                   jax.ShapeDtypeStruct((n, 2 * c), BF16),
                   jax.ShapeDtypeStruct((batch, seq // tk, c, tk), BF16)],
        compiler_params=_cparams(("parallel",), 11 * tm * c * 2, 4 * tm * c * 4),
        name="hyb_prep")(proj, proj, proj, proj, proj, proj, proj, proj, conv_w.astype(F32),
                         vec(gq), vec(gk))


def _fox_prep_body(q_ref, k_ref, v_ref, f_ref, b_ref, gq_ref, gk_ref, qt_ref, ko_ref, vt_ref,
                   tri_ref, carry_ref, *, tk, tiles_per_seq):
    tm = f_ref.shape[0]

    @pl.when(lax.rem(pl.program_id(0), tiles_per_seq) == 0)
    def _():
        carry_ref[...] = jnp.zeros_like(carry_ref)
        r = lax.broadcasted_iota(jnp.int32, (tm, tm), 0)
        cc = lax.broadcasted_iota(jnp.int32, (tm, tm), 1)
        tri_ref[...] = (r >= cc).astype(BF16)

    f = f_ref[...] + b_ref[...]
    logf = -(jnp.maximum(-f, 0.0) + jnp.log1p(jnp.exp(-jnp.abs(f))))
    tri = tri_ref[...]
    csum = sum(jnp.dot(tri, t, preferred_element_type=F32) for t in _split3(logf)) + carry_ref[0:1, :]
    carry_ref[...] = jnp.broadcast_to(csum[tm - 1:tm, :], carry_ref.shape)
    c = csum * LOG2E

    bd = _block_diag_ones(FOX_D)
    gq, gk = gq_ref[...], gk_ref[...]
    lane = lax.broadcasted_iota(jnp.int32, (1, LANES), 1)
    ones_q = jnp.where((lane >= 3) & (lane < 6), 1.0, 0.0)
    ones_k = jnp.where(lane < 3, 1.0, 0.0)
    c_hi, c_mid, c_lo = (t.astype(F32) for t in _split3(c))
    for h in range(q_ref.shape[1] // LANES):
        cols = slice(h * LANES, (h + 1) * LANES)
        _store_transposed(qt_ref, h, 0, _group_rmsnorm(q_ref[:, cols], gq, bd, FOX_D))
        ko_ref[:, 2 * h * LANES:(2 * h + 1) * LANES] = _group_rmsnorm(k_ref[:, cols], gk, bd, FOX_D).astype(BF16)
        hi, mid, lo = c_hi[:, h:h + 1], c_mid[:, h:h + 1], c_lo[:, h:h + 1]
        split_q = jnp.where(lane == 0, hi, jnp.where(lane == 1, mid, jnp.where(lane == 2, lo, 0.0)))
        split_k = jnp.where(lane == 3, hi, jnp.where(lane == 4, mid, jnp.where(lane == 5, lo, 0.0)))
        _store_transposed(qt_ref, h, LANES, split_q + ones_q)
        ko_ref[:, (2 * h + 1) * LANES:(2 * h + 2) * LANES] = (ones_k - split_k).astype(BF16)
    _store_values_transposed(vt_ref, v_ref, tk)


def _fox_prep(proj, f, b_f, gq, gk, *, batch, seq, width, tk, tm=512):
    n = proj.shape[0]
    tm = min(tm, seq)
    nt = seq // tm
    heads = width // LANES
    vec = lambda a: a.astype(F32).reshape(1, LANES)
    lane_vec = pl.BlockSpec((1, LANES), lambda i: (0, 0))
    b2 = jnp.zeros((1, LANES), F32).at[0, :b_f.shape[0]].set(b_f.astype(F32))
    return pl.pallas_call(
        functools.partial(_fox_prep_body, tk=tk, tiles_per_seq=nt),
        grid=(n // tm,),
        in_specs=[pl.BlockSpec((tm, width), lambda i: (i, 0)),
                  pl.BlockSpec((tm, width), lambda i: (i, 1)),
                  pl.BlockSpec((tm, width), lambda i: (i, 2)),
                  pl.BlockSpec((tm, LANES), lambda i: (i, 0)), lane_vec, lane_vec, lane_vec],
        out_specs=[pl.BlockSpec((1, heads, 2 * LANES, tm), lambda i: (i // nt, 0, 0, i % nt)),
                   pl.BlockSpec((tm, 2 * width), lambda i: (i, 0)),
                   pl.BlockSpec((1, tm // tk, width, tk), lambda i: (i // nt, i % nt, 0, 0))],
        out_shape=[jax.ShapeDtypeStruct((batch, heads, 2 * LANES, seq), BF16),
                   jax.ShapeDtypeStruct((n, 2 * width), BF16),
                   jax.ShapeDtypeStruct((batch, seq // tk, width, tk), BF16)],
        scratch_shapes=[pltpu.VMEM((tm, tm), BF16), pltpu.VMEM((8, LANES), F32)],
        compiler_params=_cparams(("arbitrary",), 8 * tm * width * 2 + tm * LANES * 4,
                                 2 * tm * width * 4 + tm * tm * 2),
        name="fox_prep")(proj, proj, proj, f, b2, vec(gq), vec(gk))


def _flash_core(qt_ref, k_ref, vt_ref, s_ref, p_ref, a_ref, m_ref, l_ref, acc_ref, x_ref,
                *, tk, nm, dk, shared_qv):
    qi = pl.program_id(2)
    tq = 2 * tk
    dv = acc_ref.shape[1]
    cw = min(SCORE_COLS, tk)
    all_tiles = [slice(a, a + cw) for a in range(0, tq, cw)]
    lower_tiles = [t for t in all_tiles if t.start >= tk]
    key_minus_query = (lax.broadcasted_iota(jnp.int32, (ROW_BLOCK, cw), 0)
                       - lax.broadcasted_iota(jnp.int32, (ROW_BLOCK, cw), 1))

    def visible_rows(tile, key_col):
        return tk if key_col is None else min(tk, tile.start - key_col + cw)

    def scores(c, slot, tiles=all_tiles, key_col=None):
        start = pl.multiple_of(c * tk, tk)

        def map_scores(rows, cols):
            ks = [k_ref[0, pl.ds(start, rows), m * dk:(m + 1) * dk] for m in range(nm)]
            if shared_qv:
                s = jnp.dot(jnp.concatenate(ks, axis=0), qt_ref[0, 0, :, cols],
                            preferred_element_type=F32)
                return [s[m * rows:(m + 1) * rows] for m in range(nm)]
            return [jnp.dot(ks[m], qt_ref[0, m, :, cols], preferred_element_type=F32)
                    for m in range(nm)]

        if key_col is None:
            span = slice(tiles[0].start, tiles[-1].stop)
            for m, s_all in enumerate(map_scores(tk, span)):
                for t in tiles:
                    s = s_all[:, t.start - span.start:t.stop - span.start]
                    s_ref[slot * nm + m, :, t] = s
                    mx = s[0:8]
                    for r in range(8, tk, 8):
                        mx = jnp.maximum(mx, s[r:r + 8])
                    x_ref[slot * nm + m, :, t] = jnp.max(mx, axis=0, keepdims=True)
        else:
            for t in tiles:
                rows = visible_rows(t, key_col)
                for m, s in enumerate(map_scores(rows, t)):
                    s_ref[slot * nm + m, 0:rows, t] = s

    def softmax(slot, tiles=all_tiles, key_col=None):
        for m in range(nm):
            idx = slot * nm + m
            for t in tiles:
                rows = visible_rows(t, key_col)
                rel = None if key_col is None else t.start - key_col

                def block(r):
                    s = s_ref[idx, r:r + ROW_BLOCK, t]
                    if rel is not None and r + ROW_BLOCK - 1 > rel:
                        s = jnp.where(key_minus_query <= rel - r, s, -jnp.inf)
                    return s

                if rel is None or rel >= tk:
                    smax = x_ref[idx, :, t]
                else:
                    mx = block(0)
                    for r in range(ROW_BLOCK, rows, ROW_BLOCK):
                        mx = jnp.maximum(mx, block(r))
                    smax = jnp.max(mx, axis=0, keepdims=True)
                m_old = m_ref[m, :, t]
                m_new = jnp.maximum(m_old, smax)
                alpha = jnp.exp2(m_old - m_new)
                psum = None
                for r in range(0, rows, ROW_BLOCK):
                    p = jnp.exp2(block(r) - m_new)
                    part = p[0:8]
                    for i in range(8, ROW_BLOCK, 8):
                        part = part + p[i:i + 8]
                    psum = part if psum is None else psum + part
                    p_ref[idx, r:r + ROW_BLOCK, t] = p.astype(BF16)
                m_ref[m, :, t] = m_new
                l_ref[m, :, t] = alpha * l_ref[m, :, t] + jnp.sum(psum, axis=0, keepdims=True)
                a_ref[idx, :, t] = alpha

    def weighted_values(c, slot, tiles=all_tiles, key_col=None):
        for m in range(nm):
            idx = slot * nm + m
            for t in tiles:
                rows = visible_rows(t, key_col)
                feat = slice(0, dv) if shared_qv else slice(m * dv, (m + 1) * dv)
                acc_ref[m, :, t] = (a_ref[idx, :, t] * acc_ref[m, :, t]
                                    + jnp.dot(vt_ref[0, c, feat, 0:rows], p_ref[idx, 0:rows, t],
                                              preferred_element_type=F32))

    m_ref[...] = jnp.full(m_ref.shape, -jnp.inf, F32)
    l_ref[...] = jnp.zeros(l_ref.shape, F32)
    acc_ref[...] = jnp.zeros(acc_ref.shape, F32)
    for m in range(nm):
        p_ref[nm + m] = jnp.zeros(p_ref.shape[1:], BF16)
        a_ref[nm + m] = jnp.ones(a_ref.shape[1:], F32)

    n_vis = 2 * qi
    scores(0, 0)

    def pair(u, carry):
        c = 2 * u
        scores(c + 1, 1)
        softmax(0)
        weighted_values(jnp.maximum(c - 1, 0), 1)
        scores(c + 2, 0)
        softmax(1)
        weighted_values(c, 0)
        return carry

    lax.fori_loop(0, qi, pair, 0)
    scores(n_vis + 1, 1, lower_tiles, key_col=tk)
    softmax(0, key_col=0)
    weighted_values(jnp.maximum(n_vis - 1, 0), 1)
    softmax(1, lower_tiles, key_col=tk)
    weighted_values(n_vis, 0, key_col=0)
    weighted_values(n_vis + 1, 1, lower_tiles, key_col=tk)


def _flash_scratch(tq, tk, nm, dv):
    return [pltpu.VMEM((2 * nm, tk, tq), F32),
            pltpu.VMEM((2 * nm, tk, tq), BF16),
            pltpu.VMEM((2 * nm, 1, tq), F32),
            pltpu.VMEM((nm, 1, tq), F32),
            pltpu.VMEM((nm, 1, tq), F32),
            pltpu.VMEM((nm, dv, tq), F32),
            pltpu.VMEM((2 * nm, 1, tq), F32)]


def _flash_scratch_bytes(tq, tk, nm, dv):
    return sum(_padded_bytes(b.shape, b.dtype) for b in _flash_scratch(tq, tk, nm, dv))


def _fox_attn_body(qt_ref, k_ref, vt_ref, o_ref, *scratch, tk, heads):
    _flash_core(qt_ref, k_ref, vt_ref, *scratch, tk=tk, nm=heads, dk=k_ref.shape[2] // heads,
                shared_qv=False)
    l_ref, acc_ref = scratch[4], scratch[5]
    for m in range(heads):
        o_ref[0, :, m * FOX_D:(m + 1) * FOX_D] = (acc_ref[m] / l_ref[m]).T.astype(o_ref.dtype)


def _fox_attn(qt, ka, vt, *, tk, heads_per_step=2):
    b, heads, dq, t = qt.shape
    tq = 2 * tk
    hs = heads_per_step
    return pl.pallas_call(
        functools.partial(_fox_attn_body, tk=tk, heads=hs),
        grid=(b, heads // hs, t // tq),
        in_specs=[pl.BlockSpec((1, hs, dq, tq), lambda bi, h, i: (bi, h, 0, i)),
                  pl.BlockSpec((1, t, hs * dq), lambda bi, h, i: (bi, 0, h)),
                  pl.BlockSpec((1, t // tk, hs * FOX_D, tk), lambda bi, h, i: (bi, 0, h, 0))],
        out_specs=pl.BlockSpec((1, tq, hs * FOX_D), lambda bi, h, i: (bi, i, h)),
        out_shape=jax.ShapeDtypeStruct((b, t, heads * FOX_D), BF16),
        scratch_shapes=_flash_scratch(tq, tk, hs, FOX_D),
        compiler_params=_cparams(("parallel", "parallel", "arbitrary"),
                                 hs * (dq * tq + t * dq + t * FOX_D + tq * FOX_D) * 2,
                                 _flash_scratch_bytes(tq, tk, hs, FOX_D)),
        name="fox_attn")(qt, ka, vt)


def _diff_attn_body(lq1_ref, lk1_ref, lq2_ref, lk2_ref, sg_ref, qt_ref, k_ref, vt_ref, o_ref,
                    *scratch, tk, lam_init):
    _flash_core(qt_ref, k_ref, vt_ref, *scratch, tk=tk, nm=2, dk=qt_ref.shape[2], shared_qv=True)
    l_ref, acc_ref = scratch[4], scratch[5]
    lam = (jnp.exp(jnp.sum(lq1_ref[...] * lk1_ref[...], keepdims=True))
           - jnp.exp(jnp.sum(lq2_ref[...] * lk2_ref[...], keepdims=True)) + lam_init)
    o = acc_ref[0] / l_ref[0] - lam * (acc_ref[1] / l_ref[1])
    ms = jnp.mean(o * o, axis=0, keepdims=True)
    o = ((o * lax.rsqrt(ms + EPS)) * sg_ref[...]) * (1.0 - lam_init)
    o_ref[0] = o.T.astype(o_ref.dtype)


def _diff_attn(qt, kn, vt, lq1, lk1, lq2, lk2, subln_g, *, lam_init, tk):
    b, heads, dq, t = qt.shape
    tq = 2 * tk
    dv = 2 * DIFF_D
    vec = lambda a: a.reshape(1, -1).astype(F32)
    small = lambda w: pl.BlockSpec((1, w), lambda bi, h, i: (0, 0))
    return pl.pallas_call(
        functools.partial(_diff_attn_body, tk=tk, lam_init=lam_init),
        grid=(b, heads, t // tq),
        in_specs=[small(DIFF_D), small(DIFF_D), small(DIFF_D), small(DIFF_D),
                  pl.BlockSpec((dv, 1), lambda bi, h, i: (0, 0)),
                  pl.BlockSpec((1, 1, dq, tq), lambda bi, h, i: (bi, h, 0, i)),
                  pl.BlockSpec((1, t, 2 * dq), lambda bi, h, i: (bi, 0, h)),
                  pl.BlockSpec((1, t // tk, dv, tk), lambda bi, h, i: (bi, 0, h, 0))],
        out_specs=pl.BlockSpec((1, tq, dv), lambda bi, h, i: (bi, i, h)),
        out_shape=jax.ShapeDtypeStruct((b, t, heads * dv), BF16),
        scratch_shapes=_flash_scratch(tq, tk, 2, dv),
        compiler_params=_cparams(("parallel", "parallel", "arbitrary"),
                                 (dq * tq + t * 2 * dq + t * dv + tq * dv) * 2,
                                 _flash_scratch_bytes(tq, tk, 2, dv)),
        name="diff_attn")(vec(lq1), vec(lk1), vec(lq2), vec(lk2),
                          subln_g.reshape(dv, 1).astype(F32), qt, kn, vt)


def kernel(x, norm1_g, norm2_g, hyb_w_in, hyb_conv_w, hyb_dq_g, hyb_dk_g, hyb_lq1, hyb_lk1,
           hyb_lq2, hyb_lk2, hyb_subln_g, hyb_w_out, fox_w_in, fox_b_f, fox_q_g, fox_k_g,
           fox_w_out, mlp_w1, mlp_w2):
    b, t, d = x.shape
    n = b * t
    depth = norm1_g.shape[0]
    diff_tk = fox_tk = min(ATTN_KEY_CHUNK, t // 2)
    xf = x.reshape(n, d)
    for l in range(depth):
        j = l // 2
        if l % 2 == 0:
            proj = _norm_mm(xf, norm1_g[l], _to_bf16(hyb_w_in, j))
            gq = hyb_dq_g[j].astype(F32) * (DIFF_D ** -0.5 * LOG2E)
            y_conv, qt, kn, vt = _hyb_prep(proj, hyb_conv_w[j], gq, hyb_dk_g[j], batch=b, seq=t,
                                           tk=diff_tk)
            att = _diff_attn(qt, kn.reshape(b, t, -1), vt, hyb_lq1[j], hyb_lk1[j], hyb_lq2[j],
                             hyb_lk2[j], hyb_subln_g[j], lam_init=0.8 - 0.6 * math.exp(-0.3 * l),
                             tk=diff_tk)
            xf = _mm2_res(y_conv, att.reshape(n, -1), _to_bf16(hyb_w_out, j), xf)
        else:
            w_in_t = _to_bf16(jnp.swapaxes(fox_w_in, 1, 2), j)
            wf_t = jnp.zeros((LANES, d), BF16).at[:FOX_HEADS].set(w_in_t[3 * d:])
            proj, f = _norm_mm(xf, norm1_g[l], w_in_t, nout=3 * d, wf_t=wf_t)
            gq = fox_q_g[j].astype(F32) * (FOX_D ** -0.5 * LOG2E)
            qt, ka, vt = _fox_prep(proj, f, fox_b_f[j], gq, fox_k_g[j], batch=b, seq=t, width=d,
                                   tk=fox_tk)
            att = _fox_attn(qt, ka.reshape(b, t, -1), vt, tk=fox_tk).reshape(n, -1)
            xf = _mm2_res(att, att, _to_bf16(fox_w_out, j), xf, a2_block=1)
        hid = _norm_mm(xf, norm2_g[l], _to_bf16(mlp_w1, l), sqrelu=True)
        xf = _mm_res(hid, _to_bf16(mlp_w2, l), xf)
    return xf.reshape(b, t, d)
```

```python
import functools
import math

import jax
import jax.numpy as jnp
from jax import lax
from jax.experimental import pallas as pl
from jax.experimental.pallas import tpu as pltpu

F32 = jnp.float32
BF16 = jnp.bfloat16
EPS = 1e-6
LOG2E = math.log2(math.e)
LANES = 128
BF16_SUBLANES = 16
MIB = 2 ** 20

CONV_WIDTH = 3
DIFF_HEADS = 8
DIFF_D = 64
FOX_HEADS = 16
FOX_D = 128

ATTN_KEY_CHUNK = 512
SCORE_COLS = 256
ROW_BLOCK = BF16_SUBLANES


VMEM_BYTES = 64 * MIB
VMEM_COMPILER_RESERVE = 6 * MIB


def _cparams(semantics, window_bytes, resident_bytes=0):
    limit = 2 * window_bytes + resident_bytes + VMEM_COMPILER_RESERVE
    assert limit <= VMEM_BYTES, limit
    return pltpu.CompilerParams(dimension_semantics=semantics, vmem_limit_bytes=limit)


def _padded_bytes(shape, dtype):
    item = jnp.dtype(dtype).itemsize
    sub = 8 * 4 // item
    *lead, r, c = shape
    return math.prod(lead) * (-(-r // sub) * sub) * (-(-c // LANES) * LANES) * item


def _split3(x):
    hi = x.astype(BF16)
    r1 = x - hi.astype(F32)
    mid = r1.astype(BF16)
    lo = (r1 - mid.astype(F32)).astype(BF16)
    return hi, mid, lo


CAST_BLOCK_ELEMS = 2 ** 21


def _cast_body(w_ref, o_ref):
    o_ref[...] = w_ref[0].astype(o_ref.dtype)


def _to_bf16(w_stack, layer):
    _, rows, cols = w_stack.shape
    tr = min(rows, max(BF16_SUBLANES, CAST_BLOCK_ELEMS // cols // BF16_SUBLANES * BF16_SUBLANES))
    while rows % tr:
        tr -= BF16_SUBLANES
    return pl.pallas_call(
        _cast_body,
        grid=(rows // tr,),
        in_specs=[pl.BlockSpec((1, tr, cols), lambda i: (layer, i, 0))],
        out_specs=pl.BlockSpec((tr, cols), lambda i: (i, 0)),
        out_shape=jax.ShapeDtypeStruct((rows, cols), BF16),
        compiler_params=_cparams(("parallel",), tr * cols * (4 + 2)),
        name="to_bf16")(w_stack)


def _normalize_rows(x_ref, g_ref, h_ref):
    x = x_ref[...]
    ms = jnp.mean(x * x, axis=-1, keepdims=True)
    h_ref[...] = ((x * lax.rsqrt(ms + EPS)) * g_ref[...]).astype(BF16)


def _norm_mm_body(x_ref, g_ref, w_ref, o_ref, h_ref, *, sqrelu):
    def project():
        y = jnp.dot(h_ref[...], w_ref[...], preferred_element_type=F32)
        if sqrelu:
            y = jnp.square(jnp.maximum(y, 0.0))
        o_ref[...] = y.astype(o_ref.dtype)

    @pl.when(pl.program_id(1) == 0)
    def _():
        _normalize_rows(x_ref, g_ref, h_ref)
        project()

    @pl.when(pl.program_id(1) != 0)
    def _():
        project()


def _norm_mm_gate_body(x_ref, g_ref, wt_ref, wft_ref, o_ref, f_ref, h_ref):
    contract_last = (((1,), (1,)), ((), ()))

    def project():
        y = lax.dot_general(h_ref[...], wt_ref[...], contract_last, preferred_element_type=F32)
        o_ref[...] = y.astype(o_ref.dtype)

    @pl.when(pl.program_id(1) == 0)
    def _():
        _normalize_rows(x_ref, g_ref, h_ref)
        f_ref[...] = lax.dot_general(h_ref[...], wft_ref[...], contract_last,
                                     preferred_element_type=F32)
        project()

    @pl.when(pl.program_id(1) != 0)
    def _():
        project()


def _norm_mm(x, g, w, *, nout=None, sqrelu=False, wf_t=None, tm=1024, tn=2048):
    n, d = x.shape
    nout = w.shape[1] if nout is None else nout
    tm, tn = min(tm, n), min(tn, nout)
    grid = (n // tm, nout // tn)
    x_spec = pl.BlockSpec((tm, d), lambda i, j: (i, 0))
    g_spec = pl.BlockSpec((1, d), lambda i, j: (0, 0))
    w_spec = pl.BlockSpec((d, tn), lambda i, j: (0, j))
    o_spec = pl.BlockSpec((tm, tn), lambda i, j: (i, j))
    scratch = [pltpu.VMEM((tm, d), BF16)]
    nf = 0 if wf_t is None else wf_t.shape[0]
    params = _cparams(("parallel", "arbitrary"),
                      tm * d * 4 + d * tn * 2 + tm * tn * 2 + d * nf * 2 + tm * nf * 4, tm * d * 2)
    g2 = g.reshape(1, d).astype(F32)
    if wf_t is None:
        return pl.pallas_call(
            functools.partial(_norm_mm_body, sqrelu=sqrelu),
            grid=grid, in_specs=[x_spec, g_spec, w_spec], out_specs=o_spec,
            out_shape=jax.ShapeDtypeStruct((n, nout), BF16),
            scratch_shapes=scratch, compiler_params=params, name="norm_mm")(x, g2, w)
    return pl.pallas_call(
        _norm_mm_gate_body,
        grid=grid,
        in_specs=[x_spec, g_spec, pl.BlockSpec((tn, d), lambda i, j: (j, 0)),
                  pl.BlockSpec((nf, d), lambda i, j: (0, 0))],
        out_specs=[o_spec, pl.BlockSpec((tm, nf), lambda i, j: (i, 0))],
        out_shape=[jax.ShapeDtypeStruct((n, nout), BF16), jax.ShapeDtypeStruct((n, nf), F32)],
        scratch_shapes=scratch, compiler_params=params, name="norm_mm_gate")(x, g2, w, wf_t)


def _mm_res_body(a_ref, w_ref, r_ref, o_ref):
    @pl.when(pl.program_id(2) == 0)
    def _():
        o_ref[...] = r_ref[...] + jnp.dot(a_ref[...], w_ref[...], preferred_element_type=F32)

    @pl.when(pl.program_id(2) != 0)
    def _():
        o_ref[...] += jnp.dot(a_ref[...], w_ref[...], preferred_element_type=F32)


def _mm_res(a, w, r, *, tm=1024, tn=1024, tk=4096):
    n, kdim = a.shape
    nout = w.shape[1]
    tm, tn, tk = min(tm, n), min(tn, nout), min(tk, kdim)
    return pl.pallas_call(
        _mm_res_body,
        grid=(n // tm, nout // tn, kdim // tk),
        in_specs=[pl.BlockSpec((tm, tk), lambda i, j, k: (i, k)),
                  pl.BlockSpec((tk, tn), lambda i, j, k: (k, j)),
                  pl.BlockSpec((tm, tn), lambda i, j, k: (i, j))],
        out_specs=pl.BlockSpec((tm, tn), lambda i, j, k: (i, j)),
        out_shape=jax.ShapeDtypeStruct((n, nout), F32),
        compiler_params=_cparams(("parallel", "parallel", "arbitrary"),
                                 tm * tk * 2 + tk * tn * 2 + 2 * tm * tn * 4, tm * tn * 4),
        name="mm_res")(a, w, r)


def _mm2_res_body(a1_ref, a2_ref, w_ref, r_ref, o_ref):
    k1 = a1_ref.shape[1]
    o_ref[...] = (r_ref[...]
                  + jnp.dot(a1_ref[...], w_ref[:k1, :], preferred_element_type=F32)
                  + jnp.dot(a2_ref[...], w_ref[k1:, :], preferred_element_type=F32))


def _mm2_res(a1, a2, w, r, *, a2_block=0, tm=1024):
    n = a1.shape[0]
    kdim, nout = w.shape
    k1 = k2 = kdim // 2
    tm = min(tm, n)
    return pl.pallas_call(
        _mm2_res_body,
        grid=(n // tm,),
        in_specs=[pl.BlockSpec((tm, k1), lambda i: (i, 0)),
                  pl.BlockSpec((tm, k2), lambda i: (i, a2_block)),
                  pl.BlockSpec((k1 + k2, nout), lambda i: (0, 0), pipeline_mode=pl.Buffered(1)),
                  pl.BlockSpec((tm, nout), lambda i: (i, 0))],
        out_specs=pl.BlockSpec((tm, nout), lambda i: (i, 0)),
        out_shape=jax.ShapeDtypeStruct((n, nout), F32),
        compiler_params=_cparams(("parallel",), tm * kdim * 2 + 2 * tm * nout * 4,
                                 kdim * nout * 2 + tm * nout * 4),
        name="mm2_res")(a1, a2, w, r)


def _block_diag_ones(group):
    r = (lax.broadcasted_iota(jnp.int32, (2 * LANES, LANES), 0) % LANES) // group
    c = lax.broadcasted_iota(jnp.int32, (2 * LANES, LANES), 1) // group
    return (r == c).astype(BF16)


def _group_rmsnorm(x, gain, bd, group):
    xf = x.astype(F32)
    sq = xf * xf
    hi = sq.astype(BF16)
    lo = (sq - hi.astype(F32)).astype(BF16)
    gs = jnp.dot(jnp.concatenate([hi, lo], axis=1), bd, preferred_element_type=F32)
    return (xf * lax.rsqrt(gs * (1.0 / group) + EPS)) * gain


def _store_transposed(dst_ref, head, row0, x):
    dst_ref[0, head, row0:row0 + LANES, :] = x.astype(dst_ref.dtype).T


def _store_values_transposed(vt_ref, v_ref, tk):
    tm = v_ref.shape[0]
    for h in range(v_ref.shape[1] // LANES):
        vt = v_ref[:, h * LANES:(h + 1) * LANES].T
        for ch in range(tm // tk):
            vt_ref[0, ch, h * LANES:(h + 1) * LANES, :] = vt[:, ch * tk:(ch + 1) * tk]


def _hyb_prep_body(gb_ref, gc_ref, u_ref, gcp_ref, up_ref, q_ref, k_ref, v_ref, cw_ref, gq_ref,
                   gk_ref, y_ref, qt_ref, ko_ref, vt_ref, *, tm, seq, tk):
    z = gc_ref[...].astype(F32) * u_ref[...].astype(F32)
    zp = gcp_ref[...].astype(F32) * up_ref[...].astype(F32)
    first_of_seq = lax.rem(pl.program_id(0) * tm, seq) == 0
    zp = jnp.where(first_of_seq, 0.0, zp)
    zext = jnp.concatenate([zp, z], axis=0)
    z1 = pltpu.roll(zext, 1, 0)[BF16_SUBLANES:]
    z2 = pltpu.roll(zext, 2, 0)[BF16_SUBLANES:]
    cw = cw_ref[...]
    y = gb_ref[...].astype(F32) * (cw[0:1] * z2 + cw[1:2] * z1 + cw[2:3] * z)
    y_ref[...] = y.astype(BF16)

    bd = _block_diag_ones(DIFF_D)
    gq, gk = gq_ref[...], gk_ref[...]
    low_half = lax.broadcasted_iota(jnp.int32, (1, LANES), 1) < DIFF_D
    for h in range(q_ref.shape[1] // LANES):
        cols = slice(h * LANES, (h + 1) * LANES)
        _store_transposed(qt_ref, h, 0, _group_rmsnorm(q_ref[:, cols], gq, bd, DIFF_D))
        kn = _group_rmsnorm(k_ref[:, cols], gk, bd, DIFF_D)
        ko_ref[:, 2 * h * LANES:(2 * h + 1) * LANES] = jnp.where(low_half, kn, 0.0).astype(BF16)
        ko_ref[:, (2 * h + 1) * LANES:(2 * h + 2) * LANES] = jnp.where(low_half, 0.0, kn).astype(BF16)
    _store_values_transposed(vt_ref, v_ref, tk)


def _hyb_prep(proj, conv_w, gq, gk, *, batch, seq, tk, tm=512):
    n = proj.shape[0]
    c = conv_w.shape[1]
    tm = min(tm, seq)
    nt = seq // tm
    heads = c // LANES
    halo = BF16_SUBLANES
    prev = lambda col: (lambda i: (jnp.maximum(i * (tm // halo) - 1, 0), col))
    col = lambda b: pl.BlockSpec((tm, c), lambda i: (i, b))
    vec = lambda a: jnp.tile(a.astype(F32), LANES // a.shape[0]).reshape(1, LANES)
    lane_vec = pl.BlockSpec((1, LANES), lambda i: (0, 0))
    return pl.pallas_call(
        functools.partial(_hyb_prep_body, tm=tm, seq=seq, tk=tk),
        grid=(n // tm,),
        in_specs=[col(0), col(1), col(2),
                  pl.BlockSpec((halo, c), prev(1)), pl.BlockSpec((halo, c), prev(2)),
                  col(3), col(4), col(5),
                  pl.BlockSpec((CONV_WIDTH, c), lambda i: (0, 0)), lane_vec, lane_vec],
        out_specs=[pl.BlockSpec((tm, c), lambda i: (i, 0)),
                   pl.BlockSpec((1, heads, LANES, tm), lambda i: (i // nt, 0, 0, i % nt)),
                   pl.BlockSpec((tm, 2 * c), lambda i: (i, 0)),
                   pl.BlockSpec((1, tm // tk, c, tk), lambda i: (i // nt, i % nt, 0, 0))],
        out_shape=[jax.ShapeDtypeStruct((n, c), BF16),
                   jax.ShapeDtypeStruct((batch, heads, LANES, seq), BF16),
                   jax.ShapeDtypeStruct((n, 2 * c), BF16),
                   jax.ShapeDtypeStruct((batch, seq // tk, c, tk), BF16)],
        compiler_params=_cparams(("parallel",), 11 * tm * c * 2, 4 * tm * c * 4),
        name="hyb_prep")(proj, proj, proj, proj, proj, proj, proj, proj, conv_w.astype(F32),
                         vec(gq), vec(gk))


def _fox_prep_body(q_ref, k_ref, v_ref, f_ref, b_ref, gq_ref, gk_ref, qt_ref, ko_ref, vt_ref,
                   tri_ref, carry_ref, *, tk, tiles_per_seq):
    tm = f_ref.shape[0]

    @pl.when(lax.rem(pl.program_id(0), tiles_per_seq) == 0)
    def _():
        carry_ref[...] = jnp.zeros_like(carry_ref)
        r = lax.broadcasted_iota(jnp.int32, (tm, tm), 0)
        cc = lax.broadcasted_iota(jnp.int32, (tm, tm), 1)
        tri_ref[...] = (r >= cc).astype(BF16)

    f = f_ref[...] + b_ref[...]
    logf = -(jnp.maximum(-f, 0.0) + jnp.log1p(jnp.exp(-jnp.abs(f))))
    tri = tri_ref[...]
    csum = sum(jnp.dot(tri, t, preferred_element_type=F32) for t in _split3(logf)) + carry_ref[0:1, :]
    carry_ref[...] = jnp.broadcast_to(csum[tm - 1:tm, :], carry_ref.shape)
    c = csum * LOG2E

    bd = _block_diag_ones(FOX_D)
    gq, gk = gq_ref[...], gk_ref[...]
    lane = lax.broadcasted_iota(jnp.int32, (1, LANES), 1)
    ones_q = jnp.where((lane >= 3) & (lane < 6), 1.0, 0.0)
    ones_k = jnp.where(lane < 3, 1.0, 0.0)
    c_hi, c_mid, c_lo = (t.astype(F32) for t in _split3(c))
    for h in range(q_ref.shape[1] // LANES):
        cols = slice(h * LANES, (h + 1) * LANES)
        _store_transposed(qt_ref, h, 0, _group_rmsnorm(q_ref[:, cols], gq, bd, FOX_D))
        ko_ref[:, 2 * h * LANES:(2 * h + 1) * LANES] = _group_rmsnorm(k_ref[:, cols], gk, bd, FOX_D).astype(BF16)
        hi, mid, lo = c_hi[:, h:h + 1], c_mid[:, h:h + 1], c_lo[:, h:h + 1]
        split_q = jnp.where(lane == 0, hi, jnp.where(lane == 1, mid, jnp.where(lane == 2, lo, 0.0)))
        split_k = jnp.where(lane == 3, hi, jnp.where(lane == 4, mid, jnp.where(lane == 5, lo, 0.0)))
        _store_transposed(qt_ref, h, LANES, split_q + ones_q)
        ko_ref[:, (2 * h + 1) * LANES:(2 * h + 2) * LANES] = (ones_k - split_k).astype(BF16)
    _store_values_transposed(vt_ref, v_ref, tk)


def _fox_prep(proj, f, b_f, gq, gk, *, batch, seq, width, tk, tm=512):
    n = proj.shape[0]
    tm = min(tm, seq)
    nt = seq // tm
    heads = width // LANES
    vec = lambda a: a.astype(F32).reshape(1, LANES)
    lane_vec = pl.BlockSpec((1, LANES), lambda i: (0, 0))
    b2 = jnp.zeros((1, LANES), F32).at[0, :b_f.shape[0]].set(b_f.astype(F32))
    return pl.pallas_call(
        functools.partial(_fox_prep_body, tk=tk, tiles_per_seq=nt),
        grid=(n // tm,),
        in_specs=[pl.BlockSpec((tm, width), lambda i: (i, 0)),
                  pl.BlockSpec((tm, width), lambda i: (i, 1)),
                  pl.BlockSpec((tm, width), lambda i: (i, 2)),
                  pl.BlockSpec((tm, LANES), lambda i: (i, 0)), lane_vec, lane_vec, lane_vec],
        out_specs=[pl.BlockSpec((1, heads, 2 * LANES, tm), lambda i: (i // nt, 0, 0, i % nt)),
                   pl.BlockSpec((tm, 2 * width), lambda i: (i, 0)),
                   pl.BlockSpec((1, tm // tk, width, tk), lambda i: (i // nt, i % nt, 0, 0))],
        out_shape=[jax.ShapeDtypeStruct((batch, heads, 2 * LANES, seq), BF16),
                   jax.ShapeDtypeStruct((n, 2 * width), BF16),
                   jax.ShapeDtypeStruct((batch, seq // tk, width, tk), BF16)],
        scratch_shapes=[pltpu.VMEM((tm, tm), BF16), pltpu.VMEM((8, LANES), F32)],
        compiler_params=_cparams(("arbitrary",), 8 * tm * width * 2 + tm * LANES * 4,
                                 2 * tm * width * 4 + tm * tm * 2),
        name="fox_prep")(proj, proj, proj, f, b2, vec(gq), vec(gk))


def _flash_core(qt_ref, k_ref, vt_ref, s_ref, p_ref, a_ref, m_ref, l_ref, acc_ref, x_ref,
                *, tk, nm, dk, shared_qv):
    qi = pl.program_id(2)
    tq = 2 * tk
    dv = acc_ref.shape[1]
    cw = min(SCORE_COLS, tk)
    all_tiles = [slice(a, a + cw) for a in range(0, tq, cw)]
    lower_tiles = [t for t in all_tiles if t.start >= tk]
    key_minus_query = (lax.broadcasted_iota(jnp.int32, (ROW_BLOCK, cw), 0)
                       - lax.broadcasted_iota(jnp.int32, (ROW_BLOCK, cw), 1))

    def visible_rows(tile, key_col):
        return tk if key_col is None else min(tk, tile.start - key_col + cw)

    def scores(c, slot, tiles=all_tiles, key_col=None):
        start = pl.multiple_of(c * tk, tk)

        def map_scores(rows, cols):
            ks = [k_ref[0, pl.ds(start, rows), m * dk:(m + 1) * dk] for m in range(nm)]
            if shared_qv:
                s = jnp.dot(jnp.concatenate(ks, axis=0), qt_ref[0, 0, :, cols],
                            preferred_element_type=F32)
                return [s[m * rows:(m + 1) * rows] for m in range(nm)]
            return [jnp.dot(ks[m], qt_ref[0, m, :, cols], preferred_element_type=F32)
                    for m in range(nm)]

        if key_col is None:
            span = slice(tiles[0].start, tiles[-1].stop)
            for m, s_all in enumerate(map_scores(tk, span)):
                for t in tiles:
                    s = s_all[:, t.start - span.start:t.stop - span.start]
                    s_ref[slot * nm + m, :, t] = s
                    mx = s[0:8]
                    for r in range(8, tk, 8):
                        mx = jnp.maximum(mx, s[r:r + 8])
                    x_ref[slot * nm + m, :, t] = jnp.max(mx, axis=0, keepdims=True)
        else:
            for t in tiles:
                rows = visible_rows(t, key_col)
                for m, s in enumerate(map_scores(rows, t)):
                    s_ref[slot * nm + m, 0:rows, t] = s

    def softmax(slot, tiles=all_tiles, key_col=None):
        for m in range(nm):
            idx = slot * nm + m
            for t in tiles:
                rows = visible_rows(t, key_col)
                rel = None if key_col is None else t.start - key_col

                def block(r):
                    s = s_ref[idx, r:r + ROW_BLOCK, t]
                    if rel is not None and r + ROW_BLOCK - 1 > rel:
                        s = jnp.where(key_minus_query <= rel - r, s, -jnp.inf)
                    return s

                if rel is None or rel >= tk:
                    smax = x_ref[idx, :, t]
                else:
                    mx = block(0)
                    for r in range(ROW_BLOCK, rows, ROW_BLOCK):
                        mx = jnp.maximum(mx, block(r))
                    smax = jnp.max(mx, axis=0, keepdims=True)
                m_old = m_ref[m, :, t]
                m_new = jnp.maximum(m_old, smax)
                alpha = jnp.exp2(m_old - m_new)
                psum = None
                for r in range(0, rows, ROW_BLOCK):
                    p = jnp.exp2(block(r) - m_new)
                    part = p[0:8]
                    for i in range(8, ROW_BLOCK, 8):
                        part = part + p[i:i + 8]
                    psum = part if psum is None else psum + part
                    p_ref[idx, r:r + ROW_BLOCK, t] = p.astype(BF16)
                m_ref[m, :, t] = m_new
                l_ref[m, :, t] = alpha * l_ref[m, :, t] + jnp.sum(psum, axis=0, keepdims=True)
                a_ref[idx, :, t] = alpha

    def weighted_values(c, slot, tiles=all_tiles, key_col=None):
        for m in range(nm):
            idx = slot * nm + m
            for t in tiles:
                rows = visible_rows(t, key_col)
                feat = slice(0, dv) if shared_qv else slice(m * dv, (m + 1) * dv)
                acc_ref[m, :, t] = (a_ref[idx, :, t] * acc_ref[m, :, t]
                                    + jnp.dot(vt_ref[0, c, feat, 0:rows], p_ref[idx, 0:rows, t],
                                              preferred_element_type=F32))

    m_ref[...] = jnp.full(m_ref.shape, -jnp.inf, F32)
    l_ref[...] = jnp.zeros(l_ref.shape, F32)
    acc_ref[...] = jnp.zeros(acc_ref.shape, F32)
    for m in range(nm):
        p_ref[nm + m] = jnp.zeros(p_ref.shape[1:], BF16)
        a_ref[nm + m] = jnp.ones(a_ref.shape[1:], F32)

    n_vis = 2 * qi
    scores(0, 0)

    def pair(u, carry):
        c = 2 * u
        scores(c + 1, 1)
        softmax(0)
        weighted_values(jnp.maximum(c - 1, 0), 1)
        scores(c + 2, 0)
        softmax(1)
        weighted_values(c, 0)
        return carry

    lax.fori_loop(0, qi, pair, 0)
    scores(n_vis + 1, 1, lower_tiles, key_col=tk)
    softmax(0, key_col=0)
    weighted_values(jnp.maximum(n_vis - 1, 0), 1)
    softmax(1, lower_tiles, key_col=tk)
    weighted_values(n_vis, 0, key_col=0)
    weighted_values(n_vis + 1, 1, lower_tiles, key_col=tk)


def _flash_scratch(tq, tk, nm, dv):
    return [pltpu.VMEM((2 * nm, tk, tq), F32),
            pltpu.VMEM((2 * nm, tk, tq), BF16),
            pltpu.VMEM((2 * nm, 1, tq), F32),
            pltpu.VMEM((nm, 1, tq), F32),
            pltpu.VMEM((nm, 1, tq), F32),
            pltpu.VMEM((nm, dv, tq), F32),
            pltpu.VMEM((2 * nm, 1, tq), F32)]


def _flash_scratch_bytes(tq, tk, nm, dv):
    return sum(_padded_bytes(b.shape, b.dtype) for b in _flash_scratch(tq, tk, nm, dv))


def _fox_attn_body(qt_ref, k_ref, vt_ref, o_ref, *scratch, tk, heads):
    _flash_core(qt_ref, k_ref, vt_ref, *scratch, tk=tk, nm=heads, dk=k_ref.shape[2] // heads,
                shared_qv=False)
    l_ref, acc_ref = scratch[4], scratch[5]
    for m in range(heads):
        o_ref[0, :, m * FOX_D:(m + 1) * FOX_D] = (acc_ref[m] / l_ref[m]).T.astype(o_ref.dtype)


def _fox_attn(qt, ka, vt, *, tk, heads_per_step=2):
    b, heads, dq, t = qt.shape
    tq = 2 * tk
    hs = heads_per_step
    return pl.pallas_call(
        functools.partial(_fox_attn_body, tk=tk, heads=hs),
        grid=(b, heads // hs, t // tq),
        in_specs=[pl.BlockSpec((1, hs, dq, tq), lambda bi, h, i: (bi, h, 0, i)),
                  pl.BlockSpec((1, t, hs * dq), lambda bi, h, i: (bi, 0, h)),
                  pl.BlockSpec((1, t // tk, hs * FOX_D, tk), lambda bi, h, i: (bi, 0, h, 0))],
        out_specs=pl.BlockSpec((1, tq, hs * FOX_D), lambda bi, h, i: (bi, i, h)),
        out_shape=jax.ShapeDtypeStruct((b, t, heads * FOX_D), BF16),
        scratch_shapes=_flash_scratch(tq, tk, hs, FOX_D),
        compiler_params=_cparams(("parallel", "parallel", "arbitrary"),
                                 hs * (dq * tq + t * dq + t * FOX_D + tq * FOX_D) * 2,
                                 _flash_scratch_bytes(tq, tk, hs, FOX_D)),
        name="fox_attn")(qt, ka, vt)


def _diff_attn_body(lq1_ref, lk1_ref, lq2_ref, lk2_ref, sg_ref, qt_ref, k_ref, vt_ref, o_ref,
                    *scratch, tk, lam_init):
    _flash_core(qt_ref, k_ref, vt_ref, *scratch, tk=tk, nm=2, dk=qt_ref.shape[2], shared_qv=True)
    l_ref, acc_ref = scratch[4], scratch[5]
    lam = (jnp.exp(jnp.sum(lq1_ref[...] * lk1_ref[...], keepdims=True))
           - jnp.exp(jnp.sum(lq2_ref[...] * lk2_ref[...], keepdims=True)) + lam_init)
    o = acc_ref[0] / l_ref[0] - lam * (acc_ref[1] / l_ref[1])
    ms = jnp.mean(o * o, axis=0, keepdims=True)
    o = ((o * lax.rsqrt(ms + EPS)) * sg_ref[...]) * (1.0 - lam_init)
    o_ref[0] = o.T.astype(o_ref.dtype)


def _diff_attn(qt, kn, vt, lq1, lk1, lq2, lk2, subln_g, *, lam_init, tk):
    b, heads, dq, t = qt.shape
    tq = 2 * tk
    dv = 2 * DIFF_D
    vec = lambda a: a.reshape(1, -1).astype(F32)
    small = lambda w: pl.BlockSpec((1, w), lambda bi, h, i: (0, 0))
    return pl.pallas_call(
        functools.partial(_diff_attn_body, tk=tk, lam_init=lam_init),
        grid=(b, heads, t // tq),
        in_specs=[small(DIFF_D), small(DIFF_D), small(DIFF_D), small(DIFF_D),
                  pl.BlockSpec((dv, 1), lambda bi, h, i: (0, 0)),
                  pl.BlockSpec((1, 1, dq, tq), lambda bi, h, i: (bi, h, 0, i)),
                  pl.BlockSpec((1, t, 2 * dq), lambda bi, h, i: (bi, 0, h)),
                  pl.BlockSpec((1, t // tk, dv, tk), lambda bi, h, i: (bi, 0, h, 0))],
        out_specs=pl.BlockSpec((1, tq, dv), lambda bi, h, i: (bi, i, h)),
        out_shape=jax.ShapeDtypeStruct((b, t, heads * dv), BF16),
        scratch_shapes=_flash_scratch(tq, tk, 2, dv),
        compiler_params=_cparams(("parallel", "parallel", "arbitrary"),
                                 (dq * tq + t * 2 * dq + t * dv + tq * dv) * 2,
                                 _flash_scratch_bytes(tq, tk, 2, dv)),
        name="diff_attn")(vec(lq1), vec(lk1), vec(lq2), vec(lk2),
                          subln_g.reshape(dv, 1).astype(F32), qt, kn, vt)


def kernel(x, norm1_g, norm2_g, hyb_w_in, hyb_conv_w, hyb_dq_g, hyb_dk_g, hyb_lq1, hyb_lk1,
           hyb_lq2, hyb_lk2, hyb_subln_g, hyb_w_out, fox_w_in, fox_b_f, fox_q_g, fox_k_g,
           fox_w_out, mlp_w1, mlp_w2):
    b, t, d = x.shape
    n = b * t
    depth = norm1_g.shape[0]
    diff_tk = fox_tk = min(ATTN_KEY_CHUNK, t // 2)
    xf = x.reshape(n, d)
    for l in range(depth):
        j = l // 2
        if l % 2 == 0:
            proj = _norm_mm(xf, norm1_g[l], _to_bf16(hyb_w_in, j))
            gq = hyb_dq_g[j].astype(F32) * (DIFF_D ** -0.5 * LOG2E)
            y_conv, qt, kn, vt = _hyb_prep(proj, hyb_conv_w[j], gq, hyb_dk_g[j], batch=b, seq=t,
                                           tk=diff_tk)
            att = _diff_attn(qt, kn.reshape(b, t, -1), vt, hyb_lq1[j], hyb_lk1[j], hyb_lq2[j],
                             hyb_lk2[j], hyb_subln_g[j], lam_init=0.8 - 0.6 * math.exp(-0.3 * l),
                             tk=diff_tk)
            xf = _mm2_res(y_conv, att.reshape(n, -1), _to_bf16(hyb_w_out, j), xf)
        else:
            w_in_t = _to_bf16(jnp.swapaxes(fox_w_in, 1, 2), j)
            wf_t = jnp.zeros((LANES, d), BF16).at[:FOX_HEADS].set(w_in_t[3 * d:])
            proj, f = _norm_mm(xf, norm1_g[l], w_in_t, nout=3 * d, wf_t=wf_t)
            gq = fox_q_g[j].astype(F32) * (FOX_D ** -0.5 * LOG2E)
            qt, ka, vt = _fox_prep(proj, f, fox_b_f[j], gq, fox_k_g[j], batch=b, seq=t, width=d,
                                   tk=fox_tk)
            att = _fox_attn(qt, ka.reshape(b, t, -1), vt, tk=fox_tk).reshape(n, -1)
            xf = _mm2_res(att, att, _to_bf16(fox_w_out, j), xf, a2_block=1)
        hid = _norm_mm(xf, norm2_g[l], _to_bf16(mlp_w1, l), sqrelu=True)
        xf = _mm_res(hid, _to_bf16(mlp_w2, l), xf)
    return xf.reshape(b, t, d)
```
